```python
import math
import jax, jax.numpy as jnp
from jax import lax
import numpy as np

D_MODEL = 1024
BATCH = 8
SEQ = 2048
DEPTH = 2
DEC_BATCH = 128
DEC_SEQ = 4
PAST_LEN = 16384
PAGE_SIZE = 128

D_MIX = D_MODEL
GROUP_WIDTH = D_MIX // 4
HEAD_DIM = 64
CONV_WIDTH = 4
EPS = 1e-6

SSD_WIDTH = GROUP_WIDTH
SSD_HEADS = SSD_WIDTH // HEAD_DIM
SSD_STATE = 64
SSD_GROUPS = 2
SSD_CONV_DIM = SSD_WIDTH + 2 * SSD_GROUPS * SSD_STATE
SSD_CHUNK = 64

RET_WIDTH = GROUP_WIDTH
RET_HEADS = RET_WIDTH // HEAD_DIM
RET_CHUNK = 64
ROPE_BASE = 10000.0

GLA_WIDTH = GROUP_WIDTH
GLA_HEADS = 4
GLA_KEY_WIDTH = GLA_WIDTH // 2
GLA_DK = GLA_KEY_WIDTH // GLA_HEADS
GLA_DV = GLA_WIDTH // GLA_HEADS
GLA_GATE_RANK = 16
GLA_GATE_TEMP = 16.0
GLA_CHUNK = 16

LRU_WIDTH = GROUP_WIDTH
LRU_BLOCKS = LRU_WIDTH // HEAD_DIM
LRU_BLOCK_DIM = HEAD_DIM
LRU_C = 8.0

D_FF = -(-8 * D_MODEL // (3 * 256)) * 256

IN_WIDTHS = (SSD_WIDTH, SSD_CONV_DIM, SSD_HEADS,
             RET_WIDTH, RET_WIDTH, RET_WIDTH, RET_WIDTH,
             GLA_KEY_WIDTH, GLA_KEY_WIDTH, GLA_WIDTH, GLA_GATE_RANK, GLA_WIDTH,
             LRU_WIDTH, LRU_WIDTH)
IN_DIM = sum(IN_WIDTHS)

kernel_name = 'hybrid_parallel_heads_decode_step'


def _chunk_len(l, chunk):
    return chunk if l % chunk == 0 else math.gcd(l, chunk)


def _split_points():
    return np.cumsum(np.array(IN_WIDTHS))[:-1].tolist()


def rms_norm(x, w):
    xf = x.astype(jnp.float32)
    y = xf * lax.rsqrt(jnp.mean(xf * xf, axis=-1, keepdims=True) + EPS)
    return (y * w.astype(jnp.float32)).astype(x.dtype)


def causal_conv(x, buf, w, b):
    xp = jnp.concatenate([buf.astype(x.dtype), x], axis=1)
    l = x.shape[1]
    y = sum(xp[:, j:j + l] * w[j] for j in range(CONV_WIDTH)) + b
    return y.astype(x.dtype), xp[:, -(CONV_WIDTH - 1):]


def rotary(x, pos):
    half = HEAD_DIM // 2
    inv = ROPE_BASE ** (-jnp.arange(half, dtype=jnp.float32) / half)
    ang = pos.astype(jnp.float32)[:, None] * inv[None, :]
    cos = jnp.cos(ang)[None, :, None, :]
    sin = jnp.sin(ang)[None, :, None, :]
    x1, x2 = x[..., :half], x[..., half:]
    return jnp.concatenate([x1 * cos - x2 * sin, x1 * sin + x2 * cos], axis=-1)


def chunk_scalar_decay(q, k, v, log_a, s0, chunk):
    b, l, h, dk = q.shape
    dv = v.shape[-1]
    c = _chunk_len(l, chunk)
    n = l // c
    f32 = jnp.float32
    qc = q.astype(f32).reshape(b, n, c, h, dk)
    kc = k.astype(f32).reshape(b, n, c, h, dk)
    vc = v.astype(f32).reshape(b, n, c, h, dv)
    cum = jnp.cumsum(log_a.astype(f32).reshape(b, n, c, h), axis=2)
    causal = jnp.tril(jnp.ones((c, c), dtype=bool))[None, None, :, :, None]
    decay = jnp.exp(jnp.where(causal, cum[:, :, :, None, :] - cum[:, :, None, :, :], -jnp.inf))
    scores = jnp.einsum('bnihd,bnjhd->bnijh', qc, kc) * decay
    o_intra = jnp.einsum('bnijh,bnjhv->bnihv', scores, vc)
    last = cum[:, :, -1]
    ds = jnp.einsum('bnjh,bnjhd,bnjhv->bnhdv', jnp.exp(last[:, :, None] - cum), kc, vc)

    def step(s, inp):
        ds_c, last_c = inp
        return jnp.exp(last_c)[..., None, None] * s + ds_c, s

    s_fin, s_enter = lax.scan(step, s0.astype(f32), (jnp.swapaxes(ds, 0, 1), jnp.swapaxes(last, 0, 1)))
    o_inter = jnp.einsum('bnihd,nbhdv->bnihv', qc * jnp.exp(cum)[..., None], s_enter)
    o = (o_intra + o_inter).reshape(b, l, h, dv)
    return o.astype(q.dtype), s_fin


def chunk_vector_decay(q, k, v, log_g, s0, chunk):
    b, l, h, dk = q.shape
    dv = v.shape[-1]
    c = _chunk_len(l, chunk)
    n = l // c
    f32 = jnp.float32
    qc = q.astype(f32).reshape(b, n, c, h, dk)
    kc = k.astype(f32).reshape(b, n, c, h, dk)
    vc = v.astype(f32).reshape(b, n, c, h, dv)
    cum = jnp.cumsum(log_g.astype(f32).reshape(b, n, c, h, dk), axis=2)
    causal = jnp.tril(jnp.ones((c, c), dtype=bool))[None, None, :, :, None, None]
    decay = jnp.exp(jnp.where(causal, cum[:, :, :, None] - cum[:, :, None, :], -jnp.inf))
    scores = jnp.einsum('bnihd,bnjhd,bnijhd->bnijh', qc, kc, decay)
    o_intra = jnp.einsum('bnijh,bnjhv->bnihv', scores, vc)
    last = cum[:, :, -1]
    ds = jnp.einsum('bnjhd,bnjhv->bnhdv', kc * jnp.exp(last[:, :, None] - cum), vc)

    def step(s, inp):
        ds_c, last_c = inp
        return jnp.exp(last_c)[..., None] * s + ds_c, s

    s_fin, s_enter = lax.scan(step, s0.astype(f32), (jnp.swapaxes(ds, 0, 1), jnp.swapaxes(last, 0, 1)))
    o_inter = jnp.einsum('bnihd,nbhdv->bnihv', qc * jnp.exp(cum), s_enter)
    o = (o_intra + o_inter).reshape(b, l, h, dv)
    return o.astype(q.dtype), s_fin


def ssd_mixer(z, xbc, dt_raw, conv_buf, s0, p):
    b, l = z.shape[:2]
    xbc, conv_new = causal_conv(xbc, conv_buf, p['ssd_conv_w'], p['ssd_conv_b'])
    xbc = jax.nn.silu(xbc)
    xs, bm, cm = jnp.split(xbc, [SSD_WIDTH, SSD_WIDTH + SSD_GROUPS * SSD_STATE], axis=-1)
    rep = SSD_HEADS // SSD_GROUPS
    xs = xs.reshape(b, l, SSD_HEADS, HEAD_DIM)
    bm = jnp.repeat(bm.reshape(b, l, SSD_GROUPS, SSD_STATE), rep, axis=2)
    cm = jnp.repeat(cm.reshape(b, l, SSD_GROUPS, SSD_STATE), rep, axis=2)
    dt = jax.nn.softplus(dt_raw.astype(jnp.float32) + p['ssd_dt_bias'].astype(jnp.float32))
    log_a = -dt * jnp.exp(p['ssd_a_log'].astype(jnp.float32))
    y, s_new = chunk_scalar_decay(cm, bm, xs * dt[..., None].astype(xs.dtype), log_a, s0, SSD_CHUNK)
    y = (y + xs * p['ssd_d'][:, None]).reshape(b, l, SSD_WIDTH)
    y = rms_norm(y * jax.nn.silu(z), p['ssd_norm'])
    return y, conv_new, s_new


def retention_mixer(q, k, v, g, pos, s0, p):
    b, l = q.shape[:2]
    q = rotary(q.reshape(b, l, RET_HEADS, HEAD_DIM), pos)
    k = rotary(k.reshape(b, l, RET_HEADS, HEAD_DIM), pos) * (HEAD_DIM ** -0.5)
    v = v.reshape(b, l, RET_HEADS, HEAD_DIM)
    log_gamma = jnp.log(1.0 - 2.0 ** (-5.0 - jnp.arange(RET_HEADS, dtype=jnp.float32)))
    log_a = jnp.broadcast_to(log_gamma, (b, l, RET_HEADS))
    o, s_new = chunk_scalar_decay(q, k, v, log_a, s0, RET_CHUNK)
    of = o.astype(jnp.float32)
    mu = jnp.mean(of, axis=-1, keepdims=True)
    var = jnp.mean(jnp.square(of - mu), axis=-1, keepdims=True)
    on = ((of - mu) * lax.rsqrt(var + EPS)).reshape(b, l, RET_WIDTH) * p['ret_norm'].astype(jnp.float32)
    y = on.astype(g.dtype) * jax.nn.silu(g)
    return y, s_new


def gla_mixer(q, k, v, lr, r, s0, p):
    b, l = q.shape[:2]
    log_g = jax.nn.log_sigmoid((lr @ p['gla_w_gate2'] + p['gla_b_gate']).astype(jnp.float32)) / GLA_GATE_TEMP
    q = q.reshape(b, l, GLA_HEADS, GLA_DK) * (GLA_DK ** -0.5)
    k = k.reshape(b, l, GLA_HEADS, GLA_DK)
    v = v.reshape(b, l, GLA_HEADS, GLA_DV)
    o, s_new = chunk_vector_decay(q, k, v, log_g.reshape(b, l, GLA_HEADS, GLA_DK), s0, GLA_CHUNK)
    of = o.astype(jnp.float32)
    on = of * lax.rsqrt(jnp.mean(of * of, axis=-1, keepdims=True) + EPS)
    on = on.reshape(b, l, GLA_WIDTH) * p['gla_norm'].astype(jnp.float32)
    y = on.astype(r.dtype) * jax.nn.silu(r)
    return y, s_new


def rglru_mixer(yb, xb, conv_buf, h0, p):
    b, l = xb.shape[:2]
    gate = jax.nn.gelu(yb)
    xc, conv_new = causal_conv(xb, conv_buf, p['lru_conv_w'], p['lru_conv_b'])
    xh = xc.reshape(b, l, LRU_BLOCKS, LRU_BLOCK_DIM)
    r = jax.nn.sigmoid(jnp.einsum('blhi,hij->blhj', xh, p['lru_w_a']).reshape(b, l, LRU_WIDTH) + p['lru_b_a'])
    i = jax.nn.sigmoid(jnp.einsum('blhi,hij->blhj', xh, p['lru_w_x']).reshape(b, l, LRU_WIDTH) + p['lru_b_x'])
    log_a = -LRU_C * r.astype(jnp.float32) * jax.nn.softplus(-p['lru_lambda'].astype(jnp.float32))
    a = jnp.exp(log_a)
    u = jnp.sqrt(-jnp.expm1(2.0 * log_a)) * (i * xc).astype(jnp.float32)
    u = u.at[:, 0].add(a[:, 0] * h0.astype(jnp.float32))

    def combine(e1, e2):
        a1, b1 = e1
        a2, b2 = e2
        return a1 * a2, a2 * b1 + b2

    _, h = lax.associative_scan(combine, (a, u), axis=1)
    y = h.astype(xb.dtype) * gate
    return y, conv_new, h[:, -1]


def hybrid_layer(h, pos, states, p):
    ssd_conv, ssd_s, ret_s, gla_s, lru_conv, lru_h = states
    u = rms_norm(h, p['norm_mix']) @ p['w_in']
    (ssd_z, ssd_xbc, ssd_dt, ret_q, ret_k, ret_v, ret_g,
     gla_q, gla_k, gla_v, gla_lr, gla_r, lru_y, lru_x) = jnp.split(u, _split_points(), axis=-1)
    y_ssd, ssd_conv_new, ssd_s_new = ssd_mixer(ssd_z, ssd_xbc, ssd_dt, ssd_conv, ssd_s, p)
    y_ret, ret_s_new = retention_mixer(ret_q, ret_k, ret_v, ret_g, pos, ret_s, p)
    y_gla, gla_s_new = gla_mixer(gla_q, gla_k, gla_v, gla_lr, gla_r, gla_s, p)
    y_lru, lru_conv_new, lru_h_new = rglru_mixer(lru_y, lru_x, lru_conv, lru_h, p)
    mixed = jnp.concatenate([y_ssd, y_ret, y_gla, y_lru], axis=-1) @ p['w_out']
    h = h + mixed.astype(h.dtype)
    hn = rms_norm(h, p['norm_ffn'])
    h = h + ((jax.nn.silu(hn @ p['w_gate']) * (hn @ p['w_up'])) @ p['w_down']).astype(h.dtype)
    return h, (ssd_conv_new, ssd_s_new, ret_s_new, gla_s_new, lru_conv_new, lru_h_new)


def zero_states(b, dtype):
    f32 = jnp.float32
    return (jnp.zeros((b, CONV_WIDTH - 1, SSD_CONV_DIM), dtype),
            jnp.zeros((b, SSD_HEADS, SSD_STATE, HEAD_DIM), f32),
            jnp.zeros((b, RET_HEADS, HEAD_DIM, HEAD_DIM), f32),
            jnp.zeros((b, GLA_HEADS, GLA_DK, GLA_DV), f32),
            jnp.zeros((b, CONV_WIDTH - 1, LRU_WIDTH), dtype),
            jnp.zeros((b, LRU_WIDTH), f32))


def setup_inputs(seed: int = 0) -> dict:
    key = jax.random.key(seed)
    ks = iter(jax.random.split(key, 48))
    f32 = jnp.float32

    def nrm(shape, scale):
        return jax.random.normal(next(ks), shape, f32) * scale

    def gain(shape):
        return 1.0 + nrm(shape, 0.02)

    dt0 = jnp.exp(jax.random.uniform(next(ks), (DEPTH, SSD_HEADS), f32, math.log(1e-3), math.log(1e-1)))
    a0 = jax.random.uniform(next(ks), (DEPTH, LRU_WIDTH), f32, 0.9, 0.999) ** (1.0 / LRU_C)
    return {
        'x_prompt': nrm((BATCH, SEQ, D_MODEL), 1.0),
        'x_sample': nrm((DEC_BATCH, DEC_SEQ, D_MODEL), 1.0),
        'state_ssd_conv': nrm((DEPTH, DEC_BATCH, CONV_WIDTH - 1, SSD_CONV_DIM), 1.0),
        'state_ssd': nrm((DEPTH, DEC_BATCH, SSD_HEADS, SSD_STATE, HEAD_DIM), 0.1),
        'state_ret': nrm((DEPTH, DEC_BATCH, RET_HEADS, HEAD_DIM, HEAD_DIM), 0.5),
        'state_gla': nrm((DEPTH, DEC_BATCH, GLA_HEADS, GLA_DK, GLA_DV), 0.5),
        'state_lru_conv': nrm((DEPTH, DEC_BATCH, CONV_WIDTH - 1, LRU_WIDTH), 1.0),
        'state_lru': nrm((DEPTH, DEC_BATCH, LRU_WIDTH), 0.5),
        'norm_mix': gain((DEPTH, D_MODEL)),
        'w_in': nrm((DEPTH, D_MODEL, IN_DIM), D_MODEL ** -0.5),
        'ssd_conv_w': nrm((DEPTH, CONV_WIDTH, SSD_CONV_DIM), CONV_WIDTH ** -0.5),
        'ssd_conv_b': nrm((DEPTH, SSD_CONV_DIM), 0.02),
        'ssd_dt_bias': dt0 + jnp.log(-jnp.expm1(-dt0)),
        'ssd_a_log': jnp.log(jax.random.uniform(next(ks), (DEPTH, SSD_HEADS), f32, 1.0, 16.0)),
        'ssd_d': 1.0 + nrm((DEPTH, SSD_HEADS), 0.1),
        'ssd_norm': gain((DEPTH, SSD_WIDTH)),
        'ret_norm': gain((DEPTH, RET_WIDTH)),
        'gla_w_gate2': nrm((DEPTH, GLA_GATE_RANK, GLA_KEY_WIDTH), GLA_GATE_RANK ** -0.5),
        'gla_b_gate': nrm((DEPTH, GLA_KEY_WIDTH), 0.02),
        'gla_norm': gain((DEPTH, GLA_WIDTH)),
        'lru_conv_w': nrm((DEPTH, CONV_WIDTH, LRU_WIDTH), CONV_WIDTH ** -0.5),
        'lru_conv_b': nrm((DEPTH, LRU_WIDTH), 0.02),
        'lru_w_a': nrm((DEPTH, LRU_BLOCKS, LRU_BLOCK_DIM, LRU_BLOCK_DIM), LRU_BLOCK_DIM ** -0.5),
        'lru_b_a': nrm((DEPTH, LRU_WIDTH), 0.02),
        'lru_w_x': nrm((DEPTH, LRU_BLOCKS, LRU_BLOCK_DIM, LRU_BLOCK_DIM), LRU_BLOCK_DIM ** -0.5),
        'lru_b_x': nrm((DEPTH, LRU_WIDTH), 0.02),
        'lru_lambda': jnp.log(a0) - jnp.log1p(-a0),
        'w_out': nrm((DEPTH, D_MIX, D_MODEL), D_MIX ** -0.5),
        'norm_ffn': gain((DEPTH, D_MODEL)),
        'w_gate': nrm((DEPTH, D_MODEL, D_FF), D_MODEL ** -0.5),
        'w_up': nrm((DEPTH, D_MODEL, D_FF), D_MODEL ** -0.5),
        'w_down': nrm((DEPTH, D_FF, D_MODEL), D_FF ** -0.5),
        'norm_final': gain((D_MODEL,)),
    }


def reference(x_prompt, x_sample, state_ssd_conv, state_ssd, state_ret, state_gla, state_lru_conv, state_lru,
              norm_mix, w_in, ssd_conv_w, ssd_conv_b, ssd_dt_bias, ssd_a_log, ssd_d, ssd_norm, ret_norm,
              gla_w_gate2, gla_b_gate, gla_norm, lru_conv_w, lru_conv_b, lru_w_a, lru_b_a, lru_w_x, lru_b_x,
              lru_lambda, w_out, norm_ffn, w_gate, w_up, w_down, norm_final):
    bp, lp = x_prompt.shape[:2]
    ls = x_sample.shape[1]
    pos_prompt = jnp.arange(lp, dtype=jnp.int32)
    pos_sample = PAST_LEN + jnp.arange(ls, dtype=jnp.int32)
    prompt_init = zero_states(bp, x_prompt.dtype)
    hp, hs = x_prompt, x_sample
    new_p = [[] for _ in range(6)]
    new_s = [[] for _ in range(6)]
    for i in range(DEPTH):
        p = dict(norm_mix=norm_mix[i], w_in=w_in[i], ssd_conv_w=ssd_conv_w[i], ssd_conv_b=ssd_conv_b[i],
                 ssd_dt_bias=ssd_dt_bias[i], ssd_a_log=ssd_a_log[i], ssd_d=ssd_d[i], ssd_norm=ssd_norm[i],
                 ret_norm=ret_norm[i], gla_w_gate2=gla_w_gate2[i], gla_b_gate=gla_b_gate[i], gla_norm=gla_norm[i],
                 lru_conv_w=lru_conv_w[i], lru_conv_b=lru_conv_b[i], lru_w_a=lru_w_a[i], lru_b_a=lru_b_a[i],
                 lru_w_x=lru_w_x[i], lru_b_x=lru_b_x[i], lru_lambda=lru_lambda[i], w_out=w_out[i],
                 norm_ffn=norm_ffn[i], w_gate=w_gate[i], w_up=w_up[i], w_down=w_down[i])
        hp, st_p = hybrid_layer(hp, pos_prompt, prompt_init, p)
        sample_init = (state_ssd_conv[i], state_ssd[i], state_ret[i], state_gla[i], state_lru_conv[i], state_lru[i])
        hs, st_s = hybrid_layer(hs, pos_sample, sample_init, p)
        for j in range(6):
            new_p[j].append(st_p[j])
            new_s[j].append(st_s[j])
    y_prompt = rms_norm(hp, norm_final)
    y_sample = rms_norm(hs, norm_final)
    p_ssd_conv, p_ssd, p_ret, p_gla, p_lru_conv, p_lru = [jnp.stack(a, axis=0) for a in new_p]
    s_ssd_conv, s_ssd, s_ret, s_gla, s_lru_conv, s_lru = [jnp.stack(a, axis=0) for a in new_s]
    return (y_prompt, y_sample, p_ssd_conv, p_ssd, p_ret, p_gla, p_lru_conv, p_lru,
            s_ssd_conv, s_ssd, s_ret, s_gla, s_lru_conv, s_lru)
```

```python
import functools
import math

import numpy as np
import jax
import jax.numpy as jnp
from jax import lax
from jax.experimental import pallas as pl
from jax.experimental.pallas import tpu as pltpu

F32 = jnp.float32
BF16 = jnp.bfloat16

D_MODEL = 1024
HEAD_DIM = 64
N_HEADS = 4
GROUP_WIDTH = 256
CONV_WIDTH = 4
EPS = 1e-6
SSD_CONV_DIM = 512
GLA_KEY_WIDTH = 128
GLA_DK = 32
GLA_GATE_RANK = 16
GLA_GATE_TEMP = 16.0
LRU_C = 8.0
ROPE_BASE = 10000.0
PAST_LEN = 16384
D_FF = 2816

C_Z, C_XBC, C_DT, C_RET, C_GQ, C_GK, C_GV, C_GR, C_LY, C_LX, C_MISC, IN_PACKED = (
    0, 256, 768, 1024, 2048, 2176, 2304, 2560, 2816, 3072, 3328, 3456)

CHUNK = 128
SAMPLE_SEQS = CHUNK // 4
VMEM_LIMIT_BYTES = 56 * 1024 * 1024


def _level_sizes(seg):
    sizes = []
    n = seg
    while n >= 2:
        sizes.append(n)
        n //= 2
    return sizes


@functools.lru_cache(maxsize=None)
def _chunk_constants(seg):
    c = CHUNK
    i = np.arange(c)[:, None]
    j = np.arange(c)[None, :]
    same = (i // seg) == (j // seg)
    tri = same & (j <= i)
    upper = same & (j > i)
    sizes = _level_sizes(seg)
    w_levels, up_levels, m_levels = [], [], []
    for n in sizes:
        half = n // 2
        pos = i % n
        mid = (i // n) * n + half - 1
        is_up = pos >= half
        w = np.where(is_up, (j > mid) & (j <= i), (j > i) & (j <= mid))
        w_levels.append(w)
        up_levels.append(np.broadcast_to(is_up, (c, GLA_KEY_WIDTH)))
        m_levels.append(np.tile((i // n) == (j // n), (1, N_HEADS)))
    m_levels.append(np.tile(i == j, (1, N_HEADS)))
    tu = np.concatenate([tri, upper], axis=0).astype(np.float32)
    wg = np.concatenate([tri, upper] + w_levels, axis=0).astype(np.float32)
    cmask = tri.astype(np.float32)
    gmask = np.stack(m_levels).astype(np.float32)
    gup = np.stack(up_levels).astype(np.float32)
    rows = np.arange(N_HEADS * c)[:, None]
    sm256 = ((rows // c) == (np.arange(GROUP_WIDTH)[None, :] // HEAD_DIM)).astype(np.float32)
    sm128 = ((rows // c) == (np.arange(GLA_KEY_WIDTH)[None, :] // GLA_DK)).astype(np.float32)
    rep = ((rows // N_HEADS) == j).astype(np.float32)
    rm256 = ((rows % N_HEADS) == (np.arange(GROUP_WIDTH)[None, :] // HEAD_DIM)).astype(np.float32)
    rm128 = ((rows % N_HEADS) == (np.arange(GLA_KEY_WIDTH)[None, :] // GLA_DK)).astype(np.float32)
    hh = np.arange(GROUP_WIDTH) // HEAD_DIM
    bdm = (hh[:, None] == hh[None, :]).astype(np.float32)
    bdg = ((np.arange(GLA_KEY_WIDTH) // GLA_DK)[:, None] == hh[None, :]).astype(np.float32)
    pos = (np.arange(c) % seg).astype(np.float32)
    return dict(tu=tu, wg=wg, cmask=cmask, gmask=gmask, gup=gup, sm256=sm256, sm128=sm128,
                rep=rep, rm256=rm256, rm128=rm128, bdm=bdm, bdg=bdg, pos=pos,
                tri=tri.astype(np.float32), n_levels=len(sizes))


def _const_arrays(seg):
    k = _chunk_constants(seg)
    log_gamma = jnp.log(1.0 - 2.0 ** (-5.0 - jnp.arange(N_HEADS, dtype=F32)))
    lg256 = jnp.repeat(log_gamma, HEAD_DIM)[None, :]
    pos = jnp.asarray(k["pos"])[:, None]
    retq = jnp.exp((pos + 1.0) * lg256)
    retk = jnp.exp((seg - 1.0 - pos) * lg256)
    rets = jnp.broadcast_to(jnp.exp(float(seg) * lg256), (8, GROUP_WIDTH))
    dpos = pos - pos.T
    tri = jnp.asarray(k["tri"])
    dret = jnp.concatenate(
        [jnp.where(tri > 0, jnp.exp(dpos * log_gamma[h]), 0.0) for h in range(N_HEADS)], axis=1)
    out = dict(
        tu=jnp.asarray(k["tu"], BF16), wg=jnp.asarray(k["wg"], BF16),
        cmask=jnp.asarray(k["cmask"]), gmask=jnp.asarray(k["gmask"]), gup=jnp.asarray(k["gup"]),
        retq=retq, retk=retk, rets=rets, dret=dret,
        sm256=jnp.asarray(k["sm256"], BF16), sm128=jnp.asarray(k["sm128"], BF16),
        bdm=jnp.asarray(k["bdm"]), bdg=jnp.asarray(k["bdg"]),
        mavg=jnp.asarray(k["bdm"] / HEAD_DIM, BF16),
        rep=jnp.asarray(k["rep"], BF16),
        rm256=jnp.asarray(k["rm256"], BF16), rm128=jnp.asarray(k["rm128"], BF16))
    return out, k["n_levels"]


CONST_NAMES = ("tu", "wg", "cmask", "gmask", "gup", "retq", "retk", "rets", "dret",
               "sm256", "sm128", "bdm", "bdg", "mavg")
SAMPLE_CONST_NAMES = CONST_NAMES + ("rep", "rm256", "rm128")
WEIGHT_NAMES = ("norm_mix", "w_in", "cw_ssd", "cw_lru", "vec", "bg", "w2", "wa", "wx", "w_out")

V_DTB, V_ALOG, V_SSDD, V_SSDN, V_RETN, V_GLAN, V_LBA, V_LBX, V_LLAM = range(9)


def _dot(a, b):
    return jnp.dot(a, b, preferred_element_type=F32)


def _dot_nt(a, b):
    return lax.dot_general(a, b, (((1,), (1,)), ((), ())), preferred_element_type=F32)


def _dot_tn(a, b):
    return lax.dot_general(a, b, (((0,), (0,)), ((), ())), preferred_element_type=F32)


def _split3(x):
    hi = x.astype(BF16)
    r1 = x - hi.astype(F32)
    mid = r1.astype(BF16)
    lo = (r1 - mid.astype(F32)).astype(BF16)
    return hi, mid, lo


def _dot3(w01, x):
    hi, mid, lo = _split3(x)
    return _dot(w01, hi) + _dot(w01, mid) + _dot(w01, lo)


def _dot2r(x, w):
    hi = x.astype(BF16)
    mid = (x - hi.astype(F32)).astype(BF16)
    return _dot(hi, w) + _dot(mid, w)


def _sigmoid(x):
    return 1.0 / (1.0 + jnp.exp(-x))


def _silu(x):
    return x * _sigmoid(x)


def _softplus(x):
    return jnp.maximum(x, 0.0) + jnp.log1p(jnp.exp(-jnp.abs(x)))


def _gelu_tanh(x):
    return 0.5 * x * (1.0 + jnp.tanh(math.sqrt(2.0 / math.pi) * (x + 0.044715 * (x * x * x))))


def _rms(x, w):
    return x * lax.rsqrt(jnp.mean(x * x, axis=-1, keepdims=True) + EPS) * w


def _stack4(x):
    return jnp.concatenate([x, x, x, x], axis=0)


def _shift_rows(x, k, prev_rows, seq_len):
    rows = x.shape[0]
    if seq_len is None:
        ext = jnp.concatenate([prev_rows, x], axis=0)
        return pltpu.roll(ext, k, axis=0)[8:]
    ridx = lax.broadcasted_iota(jnp.int32, x.shape, 0)
    keep = (ridx & (seq_len - 1)) >= k
    return jnp.where(keep, pltpu.roll(x, k, axis=0), prev_rows)


def _causal_conv(x, w_ref, prevs, seq_len):
    y = w_ref[4:5, :] + x * w_ref[3:4, :]
    for k in (1, 2, 3):
        prev = prevs if seq_len is None else prevs[k - 1]
        y = y + _shift_rows(x, k, prev, seq_len) * w_ref[3 - k:4 - k, :]
    return y


def _swap_halves(x):
    w = x.shape[1]
    lane = lax.broadcasted_iota(jnp.int32, x.shape, 1)
    return jnp.where((lane & 32) == 0, pltpu.roll(x, w - 32, axis=1), pltpu.roll(x, 32, axis=1))


def _expand_groups(x):
    lane = lax.broadcasted_iota(jnp.int32, x.shape, 1)
    r = pltpu.roll(x, 64, axis=1)
    first = lane < 64
    return jnp.concatenate([jnp.where(first, x, r), jnp.where(first, r, x)], axis=1)


def _head_cols(cum):
    cols = []
    for half in range(2):
        x = cum[:, half * 128:(half + 1) * 128]
        lane = lax.broadcasted_iota(jnp.int32, x.shape, 1)
        r = pltpu.roll(x, 64, axis=1)
        first = lane < 64
        cols.append(jnp.where(first, x, r))
        cols.append(jnp.where(first, r, x))
    return cols


def _scan_rows(a, u, seq_len):
    rows = a.shape[0]
    ridx = lax.broadcasted_iota(jnp.int32, a.shape, 0)
    pos = ridx if seq_len is None else (ridx & (seq_len - 1))
    limit = rows if seq_len is None else seq_len
    s = 1
    while s < limit:
        keep = pos >= s
        a_s = jnp.where(keep, pltpu.roll(a, s, axis=0), 1.0)
        u_s = jnp.where(keep, pltpu.roll(u, s, axis=0), 0.0)
        u = a * u_s + u
        a = a * a_s
        s *= 2
    return u


def _mixer_kernel(*refs, sample, n_chunks, n_levels):
    it = iter(refs)
    x_ref, cos_ref, sin_ref = next(it), next(it), next(it)
    if sample:
        e_ssd_ref, e_lru_ref, h0_ref = next(it), next(it), next(it)
        s_ssd_ref, s_ret_ref, s_gla_ref = next(it), next(it), next(it)
    W = {n: next(it) for n in WEIGHT_NAMES}
    K = {n: next(it) for n in (SAMPLE_CONST_NAMES if sample else CONST_NAMES)}
    hmid_ref = next(it)
    if sample:
        xbc_out_ref, lrux_out_ref, hall_ref = next(it), next(it), next(it)
        n_ssd_ref, n_ret_ref, n_gla_ref = next(it), next(it), next(it)
        qst_ref, kst_ref, vst_ref, ost_ref = next(it), next(it), next(it), next(it)
    else:
        tail_ssd_ref, tail_lru_ref, hl_ref = next(it), next(it), next(it)
        S_ssd_ref, S_ret_ref, S_gla_ref = next(it), next(it), next(it)

    C = CHUNK
    vec = W["vec"]
    seq_len = 4 if sample else None

    if not sample:
        @pl.when(pl.program_id(1) == 0)
        def _():
            tail_ssd_ref[...] = jnp.zeros_like(tail_ssd_ref)
            tail_lru_ref[...] = jnp.zeros_like(tail_lru_ref)
            hl_ref[...] = jnp.zeros_like(hl_ref)
            S_ssd_ref[...] = jnp.zeros_like(S_ssd_ref)
            S_ret_ref[...] = jnp.zeros_like(S_ret_ref)
            S_gla_ref[...] = jnp.zeros_like(S_gla_ref)

    x = x_ref[...]
    xn = _rms(x, W["norm_mix"][...]).astype(BF16)
    u = _dot(xn, W["w_in"][...])
    T = u.shape[0]

    ssd_z = u[:, C_Z:C_Z + 256]
    xbc_raw = u[:, C_XBC:C_XBC + 512]
    dt_raw = u[:, C_DT:C_DT + 256]
    ret_q = u[:, C_RET:C_RET + 256]
    ret_k = u[:, C_RET + 256:C_RET + 512]
    ret_v = u[:, C_RET + 512:C_RET + 768]
    ret_g = u[:, C_RET + 768:C_RET + 1024]
    gla_q = u[:, C_GQ:C_GQ + 128]
    gla_k = u[:, C_GK:C_GK + 128]
    gla_v = u[:, C_GV:C_GV + 256]
    gla_r = u[:, C_GR:C_GR + 256]
    lru_y = u[:, C_LY:C_LY + 256]
    lru_x = u[:, C_LX:C_LX + 256]
    misc = u[:, C_MISC:C_MISC + 128]

    if sample:
        prev_ssd = [e_ssd_ref[k] for k in range(3)]
        prev_lru = [e_lru_ref[k] for k in range(3)]
    else:
        prev_ssd = tail_ssd_ref[...]
        prev_lru = tail_lru_ref[...]
    xbc = _silu(_causal_conv(xbc_raw, W["cw_ssd"], prev_ssd, seq_len))
    xs = xbc[:, :256]
    ssd_k = _expand_groups(xbc[:, 256:384])
    ssd_q = _expand_groups(xbc[:, 384:512])
    dt = _softplus(dt_raw + vec[V_DTB:V_DTB + 1, :])
    ssd_la = -dt * jnp.exp(vec[V_ALOG:V_ALOG + 1, :])
    ssd_v = xs * dt

    cos = cos_ref[...]
    sin = sin_ref[...]
    rq = ret_q * cos + _swap_halves(ret_q) * sin
    rk = (ret_k * cos + _swap_halves(ret_k) * sin) * (HEAD_DIM ** -0.5)

    gate_x = _dot(misc.astype(BF16), W["w2"][...]) + W["bg"][0:1, :]
    gla_lg = (jnp.minimum(gate_x, 0.0) - jnp.log1p(jnp.exp(-jnp.abs(gate_x)))) / GLA_GATE_TEMP
    gq = gla_q * (GLA_DK ** -0.5)

    xc = _causal_conv(lru_x, W["cw_lru"], prev_lru, seq_len)
    xcb = xc.astype(BF16)
    lr = _sigmoid(_dot(xcb, W["wa"][...]) + vec[V_LBA:V_LBA + 1, :])
    li = _sigmoid(_dot(xcb, W["wx"][...]) + vec[V_LBX:V_LBX + 1, :])
    l_loga = -LRU_C * lr * _softplus(-vec[V_LLAM:V_LLAM + 1, :])
    l_a = jnp.exp(l_loga)
    l_u = jnp.sqrt(-jnp.tanh(l_loga) * (l_a * l_a + 1.0)) * (li * xc)
    if sample:
        l_u = l_u + l_a * h0_ref[...]
    else:
        first = lax.broadcasted_iota(jnp.int32, l_u.shape, 0) == 0
        l_u = l_u + jnp.where(first, l_a * hl_ref[7:8, :], 0.0)
    l_h = _scan_rows(l_a, l_u, seq_len)
    y_lru = l_h * _gelu_tanh(lru_y)

    cmask = K["cmask"][...] > 0.0
    sm256 = K["sm256"][...]
    sm128 = K["sm128"][...]

    def stack_kv(a, mask):
        return _stack4(a.astype(BF16)) * mask

    def sample_states(qt, kt, v, a_last, s_ref, n_ref, dk, rm):
        rep = K["rep"][...]
        qst_ref[:, :dk] = (_dot(rep, qt.astype(BF16)).astype(BF16) * rm)
        kst_ref[:, :dk] = (_dot(rep, kt.astype(BF16)).astype(BF16) * rm)
        for h in range(N_HEADS):
            vst_ref[pl.ds(h, C, stride=N_HEADS), :] = v[:, h * HEAD_DIM:(h + 1) * HEAD_DIM]
        a_t = jnp.concatenate([a_last[:, i * 128:(i + 1) * 128].T for i in range(dk // 128)], axis=0)
        lane = lax.broadcasted_iota(jnp.int32, a_t.shape, 1)

        def body(j, carry):
            r0 = pl.multiple_of(j * 16, 16)
            s0 = s_ref[j]
            ost_ref[pl.ds(r0, 16), :] = _dot(qst_ref[pl.ds(r0, 16), :dk], s0.astype(BF16))
            ds = _dot_tn(kst_ref[pl.ds(r0, 16), :dk], vst_ref[pl.ds(r0, 16), :].astype(BF16))
            col = jnp.sum(jnp.where(lane == 4 * j + 3, a_t, 0.0), axis=1, keepdims=True)
            n_ref[j] = col * s0 + ds
            return carry

        lax.fori_loop(0, SAMPLE_SEQS, body, 0)
        return jnp.concatenate(
            [ost_ref[pl.ds(h, C, stride=N_HEADS), :] for h in range(N_HEADS)], axis=1)

    o_ssd, o_ret, o_gla = [], [], []
    for c in range(n_chunks):
        sl = slice(c * C, (c + 1) * C)

        e = _dot3(K["tu"][...], ssd_la[sl])
        cum, rem = e[:C], e[C:]
        q, k, v = ssd_q[sl], ssd_k[sl], ssd_v[sl]
        dec = jnp.concatenate(
            [jnp.exp(jnp.where(cmask, col - col.T, -jnp.inf)) for col in _head_cols(cum)], axis=1)
        p = _dot_nt(q.astype(BF16), stack_kv(k, sm256)) * dec
        o = _dot(p.astype(BF16), stack_kv(v, sm256))
        qt = q * jnp.exp(cum)
        kt = k * jnp.exp(rem)
        if sample:
            o = o + sample_states(qt, kt, v, jnp.exp(cum), s_ssd_ref, n_ssd_ref, 256, K["rm256"][...])
        else:
            s_old = S_ssd_ref[...]
            o = o + _dot(qt.astype(BF16), s_old.astype(BF16))
            ds = _dot_tn(kt.astype(BF16), v.astype(BF16)) * K["bdm"][...]
            a_chunk = jnp.exp(jnp.sum(ssd_la[sl], axis=0, keepdims=True))
            S_ssd_ref[...] = s_old * a_chunk + ds
        o_ssd.append(o)

        q, k, v = rq[sl], rk[sl], ret_v[sl]
        p = _dot_nt(q.astype(BF16), stack_kv(k, sm256)) * K["dret"][...]
        o = _dot(p.astype(BF16), stack_kv(v, sm256))
        qt = q * K["retq"][...]
        kt = k * K["retk"][...]
        if sample:
            o = o + sample_states(qt, kt, v, K["retq"][...], s_ret_ref, n_ret_ref, 256, K["rm256"][...])
        else:
            s_old = S_ret_ref[...]
            o = o + _dot(qt.astype(BF16), s_old.astype(BF16))
            ds = _dot_tn(kt.astype(BF16), v.astype(BF16)) * K["bdm"][...]
            S_ret_ref[...] = s_old * K["rets"][0:1, :] + ds
        o_ret.append(o)

        e = _dot3(K["wg"][...], gla_lg[sl])
        cum, rem = e[:C], e[C:2 * C]
        q, k, v = gq[sl], gla_k[sl], gla_v[sl]
        p = _dot_nt(q.astype(BF16), stack_kv(k, sm128)) * K["gmask"][n_levels]
        for lvl in range(n_levels):
            f = jnp.exp(e[(2 + lvl) * C:(3 + lvl) * C])
            up = K["gup"][lvl] > 0.0
            ql = jnp.where(up, q * f, 0.0)
            kl = jnp.where(up, 0.0, k * f)
            p = p + _dot_nt(ql.astype(BF16), stack_kv(kl, sm128)) * K["gmask"][lvl]
        o = _dot(p.astype(BF16), stack_kv(v, sm256))
        a_cum = jnp.exp(cum)
        qt = q * a_cum
        kt = k * jnp.exp(rem)
        if sample:
            o = o + sample_states(qt, kt, v, a_cum, s_gla_ref, n_gla_ref, 128, K["rm128"][...])
        else:
            s_old = S_gla_ref[...]
            o = o + _dot(qt.astype(BF16), s_old.astype(BF16))
            ds = _dot_tn(kt.astype(BF16), v.astype(BF16)) * K["bdg"][...]
            a_col = a_cum.T[:, C - 1:C]
            S_gla_ref[...] = s_old * a_col + ds
        o_gla.append(o)

    cat = (lambda xs_: xs_[0]) if n_chunks == 1 else (lambda xs_: jnp.concatenate(xs_, axis=0))
    o_ssd, o_ret, o_gla = cat(o_ssd), cat(o_ret), cat(o_gla)

    mavg = K["mavg"][...]
    y_ssd = (o_ssd + xs * vec[V_SSDD:V_SSDD + 1, :]) * _silu(ssd_z)
    y_ssd = _rms(y_ssd, vec[V_SSDN:V_SSDN + 1, :])
    mu = _dot2r(o_ret, mavg)
    d = o_ret - mu
    var = _dot2r(d * d, mavg)
    y_ret = d * lax.rsqrt(var + EPS) * vec[V_RETN:V_RETN + 1, :] * _silu(ret_g)
    ms = _dot2r(o_gla * o_gla, mavg)
    y_gla = o_gla * lax.rsqrt(ms + EPS) * vec[V_GLAN:V_GLAN + 1, :] * _silu(gla_r)

    ymix = jnp.concatenate([y_ssd, y_ret, y_gla, y_lru], axis=1).astype(BF16)
    hmid_ref[...] = x + _dot(ymix, W["w_out"][...])

    if sample:
        xbc_out_ref[...] = xbc_raw
        lrux_out_ref[...] = lru_x
        hall_ref[...] = l_h
    else:
        tail_ssd_ref[...] = xbc_raw[T - 8:, :]
        tail_lru_ref[...] = lru_x[T - 8:, :]
        hl_ref[...] = l_h[T - 8:, :]


def _ffn_kernel(x_ref, nf_ref, wg_ref, wu_ref, wd_ref, nfin_ref, o_ref, *, final_norm):
    x = x_ref[...]
    hn = _rms(x, nf_ref[...]).astype(BF16)
    g = _dot(hn, wg_ref[...])
    up = _dot(hn, wu_ref[...])
    act = (_silu(g) * up).astype(BF16)
    y = x + _dot(act, wd_ref[...])
    if final_norm:
        y = _rms(y, nfin_ref[...])
    o_ref[...] = y


def _ffn(h2d, norm_ffn, wg, wu, wd, norm_final, final_norm, tm):
    rows = h2d.shape[0]
    full = lambda a: pl.BlockSpec(a.shape, lambda i: (0,) * a.ndim)
    return pl.pallas_call(
        functools.partial(_ffn_kernel, final_norm=final_norm),
        out_shape=jax.ShapeDtypeStruct(h2d.shape, F32),
        grid=(rows // tm,),
        in_specs=[pl.BlockSpec((tm, D_MODEL), lambda i: (i, 0)),
                  full(norm_ffn), full(wg), full(wu), full(wd), full(norm_final)],
        out_specs=pl.BlockSpec((tm, D_MODEL), lambda i: (i, 0)),
        compiler_params=pltpu.CompilerParams(
            dimension_semantics=("arbitrary",), vmem_limit_bytes=VMEM_LIMIT_BYTES),
        name="ffn",
    )(h2d, norm_ffn, wg, wu, wd, norm_final)


def _pack_layer_weights(i, p):
    w = p["w_in"][i]
    w_in = jnp.concatenate([
        w[:, 0:768],
        jnp.repeat(w[:, 768:772], HEAD_DIM, axis=1),
        w[:, 772:1796],
        w[:, 1796:2308],
        w[:, 2324:2580],
        w[:, 2580:3092],
        w[:, 2308:2324],
        jnp.zeros((D_MODEL, 128 - GLA_GATE_RANK), F32)], axis=1).astype(BF16)
    pad3 = lambda a: jnp.concatenate([a, jnp.zeros((3, a.shape[1]), F32)], axis=0)
    cw_ssd = pad3(jnp.concatenate([p["ssd_conv_w"][i], p["ssd_conv_b"][i][None]], axis=0))
    cw_lru = pad3(jnp.concatenate([p["lru_conv_w"][i], p["lru_conv_b"][i][None]], axis=0))
    rep = lambda a: jnp.repeat(a, HEAD_DIM)
    vec = jnp.stack([rep(p["ssd_dt_bias"][i]), rep(p["ssd_a_log"][i]), rep(p["ssd_d"][i]),
                     p["ssd_norm"][i], p["ret_norm"][i], p["gla_norm"][i],
                     p["lru_b_a"][i], p["lru_b_x"][i], p["lru_lambda"][i]], axis=0)
    vec = jnp.concatenate([vec, jnp.zeros((16 - vec.shape[0], GROUP_WIDTH), F32)], axis=0)
    bg = jnp.concatenate([p["gla_b_gate"][i][None], jnp.zeros((7, GLA_KEY_WIDTH), F32)], axis=0)
    w2 = jnp.concatenate([p["gla_w_gate2"][i],
                          jnp.zeros((128 - GLA_GATE_RANK, GLA_KEY_WIDTH), F32)], axis=0).astype(BF16)
    bd = lambda a: jax.scipy.linalg.block_diag(*[a[h] for h in range(N_HEADS)]).astype(BF16)
    return dict(norm_mix=p["norm_mix"][i][None], w_in=w_in, cw_ssd=cw_ssd, cw_lru=cw_lru,
                vec=vec, bg=bg, w2=w2, wa=bd(p["lru_w_a"][i]), wx=bd(p["lru_w_x"][i]),
                w_out=p["w_out"][i].astype(BF16))


def _rope_tables(pos):
    half = HEAD_DIM // 2
    inv = ROPE_BASE ** (-jnp.arange(half, dtype=F32) / half)
    ang = pos.astype(F32)[:, None] * inv[None, :]
    cos, sin = jnp.cos(ang), jnp.sin(ang)
    cos = jnp.tile(jnp.concatenate([cos, cos], axis=1), (1, N_HEADS))
    sin = jnp.tile(jnp.concatenate([-sin, sin], axis=1), (1, N_HEADS))
    return cos, sin


def _full_spec(a, grid_rank):
    zeros = (0,) * a.ndim
    if grid_rank == 1:
        return pl.BlockSpec(a.shape, lambda i: zeros)
    return pl.BlockSpec(a.shape, lambda b, t: zeros)


def _prompt_mixer(h, cos, sin, wts, consts, n_levels, tq):
    b, l, _ = h.shape
    weights = [wts[n] for n in WEIGHT_NAMES]
    cvals = [consts[n] for n in CONST_NAMES]
    row = lambda w: pl.BlockSpec((None, tq, w), lambda bi, ti: (bi, ti, 0))
    per_seq = lambda r, w: pl.BlockSpec((None, r, w), lambda bi, ti: (bi, 0, 0))
    out_shapes = (jax.ShapeDtypeStruct((b, l, D_MODEL), F32),
                  jax.ShapeDtypeStruct((b, 8, SSD_CONV_DIM), F32),
                  jax.ShapeDtypeStruct((b, 8, GROUP_WIDTH), F32),
                  jax.ShapeDtypeStruct((b, 8, GROUP_WIDTH), F32),
                  jax.ShapeDtypeStruct((b, 256, 256), F32),
                  jax.ShapeDtypeStruct((b, 256, 256), F32),
                  jax.ShapeDtypeStruct((b, 128, 256), F32))
    return pl.pallas_call(
        functools.partial(_mixer_kernel, sample=False, n_chunks=tq // CHUNK, n_levels=n_levels),
        out_shape=out_shapes,
        grid=(b, l // tq),
        in_specs=[row(D_MODEL),
                  pl.BlockSpec((tq, GROUP_WIDTH), lambda bi, ti: (ti, 0)),
                  pl.BlockSpec((tq, GROUP_WIDTH), lambda bi, ti: (ti, 0))]
                 + [_full_spec(a, 2) for a in weights + cvals],
        out_specs=(row(D_MODEL), per_seq(8, SSD_CONV_DIM), per_seq(8, GROUP_WIDTH),
                   per_seq(8, GROUP_WIDTH), per_seq(256, 256), per_seq(256, 256), per_seq(128, 256)),
        compiler_params=pltpu.CompilerParams(
            dimension_semantics=("arbitrary", "arbitrary"), vmem_limit_bytes=VMEM_LIMIT_BYTES),
        name="prompt_mixer",
    )(h, cos, sin, *weights, *cvals)


def _sample_mixer(h2d, cos, sin, e_ssd, e_lru, h0, s_ssd, s_ret, s_gla, wts, consts, n_levels):
    rows = h2d.shape[0]
    nseq = rows // 4
    c = CHUNK
    weights = [wts[n] for n in WEIGHT_NAMES]
    cvals = [consts[n] for n in SAMPLE_CONST_NAMES]
    row = lambda w: pl.BlockSpec((c, w), lambda i: (i, 0))
    st = lambda dk: pl.BlockSpec((SAMPLE_SEQS, dk, HEAD_DIM), lambda i: (i, 0, 0))
    st_out = lambda dk: pl.BlockSpec((SAMPLE_SEQS, dk, HEAD_DIM), lambda i: (i, 0, 0),
                                     pipeline_mode=pl.Buffered(1))
    out_shapes = (jax.ShapeDtypeStruct((rows, D_MODEL), F32),
                  jax.ShapeDtypeStruct((rows, SSD_CONV_DIM), F32),
                  jax.ShapeDtypeStruct((rows, GROUP_WIDTH), F32),
                  jax.ShapeDtypeStruct((rows, GROUP_WIDTH), F32),
                  jax.ShapeDtypeStruct((nseq, 256, HEAD_DIM), F32),
                  jax.ShapeDtypeStruct((nseq, 256, HEAD_DIM), F32),
                  jax.ShapeDtypeStruct((nseq, 128, HEAD_DIM), F32))
    return pl.pallas_call(
        functools.partial(_mixer_kernel, sample=True, n_chunks=1, n_levels=n_levels),
        out_shape=out_shapes,
        grid=(rows // c,),
        in_specs=[row(D_MODEL), row(GROUP_WIDTH), row(GROUP_WIDTH),
                  pl.BlockSpec((3, c, SSD_CONV_DIM), lambda i: (0, i, 0)),
                  pl.BlockSpec((3, c, GROUP_WIDTH), lambda i: (0, i, 0)),
                  row(GROUP_WIDTH), st(256), st(256), st(128)]
                 + [_full_spec(a, 1) for a in weights + cvals],
        out_specs=(row(D_MODEL), row(SSD_CONV_DIM), row(GROUP_WIDTH), row(GROUP_WIDTH),
                   st_out(256), st_out(256), st_out(128)),
        scratch_shapes=[pltpu.VMEM((N_HEADS * c, 256), BF16), pltpu.VMEM((N_HEADS * c, 256), BF16),
                        pltpu.VMEM((N_HEADS * c, HEAD_DIM), F32), pltpu.VMEM((N_HEADS * c, HEAD_DIM), F32)],
        compiler_params=pltpu.CompilerParams(
            dimension_semantics=("arbitrary",), vmem_limit_bytes=VMEM_LIMIT_BYTES),
        name="sample_mixer",
    )(h2d, cos, sin, e_ssd, e_lru, h0, s_ssd, s_ret, s_gla, *weights, *cvals)


def _incoming_rows(state):
    b, _, ch = state.shape
    outs = []
    for k in (1, 2, 3):
        rows = [state[:, 3 + t - k, :] if t < k else jnp.zeros((b, ch), F32) for t in range(4)]
        outs.append(jnp.stack(rows, axis=1).reshape(b * 4, ch))
    return jnp.stack(outs, axis=0)


def _diag_blocks(s, dk):
    return jnp.stack([s[:, h * dk:(h + 1) * dk, h * HEAD_DIM:(h + 1) * HEAD_DIM]
                      for h in range(N_HEADS)], axis=1)


def kernel(x_prompt, x_sample, state_ssd_conv, state_ssd, state_ret, state_gla, state_lru_conv, state_lru,
           norm_mix, w_in, ssd_conv_w, ssd_conv_b, ssd_dt_bias, ssd_a_log, ssd_d, ssd_norm, ret_norm,
           gla_w_gate2, gla_b_gate, gla_norm, lru_conv_w, lru_conv_b, lru_w_a, lru_b_a, lru_w_x, lru_b_x,
           lru_lambda, w_out, norm_ffn, w_gate, w_up, w_down, norm_final):
    params = dict(norm_mix=norm_mix, w_in=w_in, ssd_conv_w=ssd_conv_w, ssd_conv_b=ssd_conv_b,
                  ssd_dt_bias=ssd_dt_bias, ssd_a_log=ssd_a_log, ssd_d=ssd_d, ssd_norm=ssd_norm,
                  ret_norm=ret_norm, gla_w_gate2=gla_w_gate2, gla_b_gate=gla_b_gate, gla_norm=gla_norm,
                  lru_conv_w=lru_conv_w, lru_conv_b=lru_conv_b, lru_w_a=lru_w_a, lru_b_a=lru_b_a,
                  lru_w_x=lru_w_x, lru_b_x=lru_b_x, lru_lambda=lru_lambda, w_out=w_out)
    depth = w_in.shape[0]
    bp, lp, _ = x_prompt.shape
    bs, ls, _ = x_sample.shape
    assert ls == 4 and (bs * ls) % CHUNK == 0 and lp % CHUNK == 0
    tq = 256 if lp % 256 == 0 else CHUNK
    tm_p = 512 if (bp * lp) % 512 == 0 else CHUNK
    tm_s = 256 if (bs * ls) % 256 == 0 else CHUNK

    consts_p, nl_p = _const_arrays(CHUNK)
    consts_s, nl_s = _const_arrays(4)
    cos_p, sin_p = _rope_tables(jnp.arange(lp, dtype=jnp.int32))
    cos_s, sin_s = _rope_tables(PAST_LEN + jnp.arange(ls, dtype=jnp.int32))
    cos_s, sin_s = jnp.tile(cos_s, (bs, 1)), jnp.tile(sin_s, (bs, 1))
    nfin = norm_final[None]

    hp = x_prompt
    hs = x_sample.reshape(bs * ls, D_MODEL)
    new_p = [[] for _ in range(6)]
    new_s = [[] for _ in range(6)]
    for i in range(depth):
        wts = _pack_layer_weights(i, params)
        wg, wu, wd = w_gate[i].astype(BF16), w_up[i].astype(BF16), w_down[i].astype(BF16)
        nf = norm_ffn[i][None]
        last = i == depth - 1

        hmid, tail_ssd, tail_lru, hl, s_ssd, s_ret, s_gla = _prompt_mixer(
            hp, cos_p, sin_p, wts, consts_p, nl_p, tq)
        hp = _ffn(hmid.reshape(bp * lp, D_MODEL), nf, wg, wu, wd, nfin, last, tm_p).reshape(bp, lp, D_MODEL)
        for j, a in enumerate((tail_ssd[:, 5:], _diag_blocks(s_ssd, 64), _diag_blocks(s_ret, 64),
                               _diag_blocks(s_gla, 32), tail_lru[:, 5:], hl[:, 7])):
            new_p[j].append(a)

        h0 = jnp.concatenate([state_lru[i][:, None, :], jnp.zeros((bs, 3, GROUP_WIDTH), F32)],
                             axis=1).reshape(bs * ls, GROUP_WIDTH)
        hmid, xbc_raw, lrux_raw, hall, n_ssd, n_ret, n_gla = _sample_mixer(
            hs, cos_s, sin_s, _incoming_rows(state_ssd_conv[i]), _incoming_rows(state_lru_conv[i]), h0,
            state_ssd[i].reshape(bs, 256, HEAD_DIM), state_ret[i].reshape(bs, 256, HEAD_DIM),
            state_gla[i].reshape(bs, 128, HEAD_DIM), wts, consts_s, nl_s)
        hs = _ffn(hmid, nf, wg, wu, wd, nfin, last, tm_s)
        for j, a in enumerate((xbc_raw.reshape(bs, ls, -1)[:, 1:], n_ssd.reshape(bs, 4, 64, 64),
                               n_ret.reshape(bs, 4, 64, 64), n_gla.reshape(bs, 4, 32, 64),
                               lrux_raw.reshape(bs, ls, -1)[:, 1:], hall.reshape(bs, ls, -1)[:, 3])):
            new_s[j].append(a)

    outs_p = [jnp.stack(a, axis=0) for a in new_p]
    outs_s = [jnp.stack(a, axis=0) for a in new_s]
    return (hp, hs.reshape(bs, ls, D_MODEL), *outs_p, *outs_s)
```

```python
import functools
import math

import numpy as np
import jax
import jax.numpy as jnp
from jax import lax
from jax.experimental import pallas as pl
from jax.experimental.pallas import tpu as pltpu

F32 = jnp.float32
BF16 = jnp.bfloat16

D_MODEL = 1024
HEAD_DIM = 64
N_HEADS = 4
GROUP_WIDTH = 256
EPS = 1e-6
SSD_CONV_DIM = 512
GLA_KEY_WIDTH = 128
GLA_DK = 32
GLA_GATE_RANK = 16
GLA_GATE_TEMP = 16.0
LRU_C = 8.0
ROPE_BASE = 10000.0
PAST_LEN = 16384
IN_DIM = 3092

C_Z, C_XBC, C_DT, C_RET, C_GQ, C_GK, C_GV, C_GR, C_LY, C_LX, C_MISC, IN_PACKED = (
    0, 256, 768, 1024, 2048, 2176, 2304, 2560, 2816, 3072, 3328, 3456)
S_DT, S_RET, S_GLA, S_GLR, S_GR, S_LRU = 768, 772, 1796, 2308, 2324, 2580

CHUNK = 128
SAMPLE_SEQS = CHUNK // 4
VMEM_LIMIT_BYTES = 56 * 1024 * 1024


def _level_sizes(seg):
    sizes = []
    n = seg
    while n >= 2:
        sizes.append(n)
        n //= 2
    return tuple(sizes)


@functools.lru_cache(maxsize=None)
def _chunk_constants(seg):
    c = CHUNK
    i = np.arange(c)[:, None]
    j = np.arange(c)[None, :]
    same = (i // seg) == (j // seg)
    tri = same & (j <= i)
    upper = same & (j > i)
    up_levels, m_levels = [], []
    for n in _level_sizes(seg):
        up_levels.append(np.broadcast_to((i % n) >= n // 2, (c, GLA_KEY_WIDTH)))
        m_levels.append(np.tile((i // n) == (j // n), (1, N_HEADS)))
    m_levels.append(np.tile(i == j, (1, N_HEADS)))
    rows = np.arange(N_HEADS * c)[:, None]
    hh = np.arange(GROUP_WIDTH) // HEAD_DIM
    return dict(
        tu=np.concatenate([tri, upper], axis=0).astype(np.float32),
        cmask=tri.astype(np.float32),
        gmask=np.stack(m_levels).astype(np.float32),
        gup=np.stack(up_levels).astype(np.float32),
        sm256=((rows // c) == hh[None, :]).astype(np.float32),
        sm128=((rows // c) == (np.arange(GLA_KEY_WIDTH)[None, :] // GLA_DK)).astype(np.float32),
        rep=((rows // N_HEADS) == j).astype(np.float32),
        rm256=((rows % N_HEADS) == hh[None, :]).astype(np.float32),
        rm128=((rows % N_HEADS) == (np.arange(GLA_KEY_WIDTH)[None, :] // GLA_DK)).astype(np.float32),
        bdm=(hh[:, None] == hh[None, :]).astype(np.float32),
        bdg=((np.arange(GLA_KEY_WIDTH) // GLA_DK)[:, None] == hh[None, :]).astype(np.float32),
        pos=(np.arange(c) % seg).astype(np.float32))


def _const_arrays(seg):
    k = _chunk_constants(seg)
    log_gamma = jnp.log(1.0 - 2.0 ** (-5.0 - jnp.arange(N_HEADS, dtype=F32)))
    lg256 = jnp.repeat(log_gamma, HEAD_DIM)[None, :]
    pos = jnp.asarray(k["pos"])[:, None]
    dpos = pos - pos.T
    tri = jnp.asarray(k["cmask"])
    dret = jnp.concatenate(
        [jnp.where(tri > 0, jnp.exp(dpos * log_gamma[h]), 0.0) for h in range(N_HEADS)], axis=1)
    return dict(
        tu=jnp.asarray(k["tu"], BF16),
        cmask=tri, gmask=jnp.asarray(k["gmask"]), gup=jnp.asarray(k["gup"]),
        retq=jnp.exp((pos + 1.0) * lg256), retk=jnp.exp((seg - 1.0 - pos) * lg256),
        rets=jnp.broadcast_to(jnp.exp(float(seg) * lg256), (8, GROUP_WIDTH)), dret=dret,
        sm256=jnp.asarray(k["sm256"], BF16), sm128=jnp.asarray(k["sm128"], BF16),
        bdm=jnp.asarray(k["bdm"]), bdg=jnp.asarray(k["bdg"]),
        mavg=jnp.asarray(k["bdm"] / HEAD_DIM, BF16),
        rep=jnp.asarray(k["rep"], BF16),
        rm256=jnp.asarray(k["rm256"], BF16), rm128=jnp.asarray(k["rm128"], BF16))


CONST_NAMES = ("tu", "cmask", "gmask", "gup", "retq", "retk", "rets", "dret",
               "sm256", "sm128", "bdm", "bdg", "mavg")
SAMPLE_CONST_NAMES = CONST_NAMES + ("rep", "rm256", "rm128")
WEIGHT_NAMES = ("norm_mix", "w_in", "cw_ssd", "cw_lru", "vec", "bg", "w2", "wa", "wx", "w_out")

V_DTB, V_ALOG, V_SSDD, V_SSDN, V_RETN, V_GLAN, V_LBA, V_LBX, V_LLAM = range(9)


def _dot(a, b):
    return jnp.dot(a, b, preferred_element_type=F32)


def _dot_nt(a, b):
    return lax.dot_general(a, b, (((1,), (1,)), ((), ())), preferred_element_type=F32)


def _dot_tn(a, b):
    return lax.dot_general(a, b, (((0,), (0,)), ((), ())), preferred_element_type=F32)


def _split3(x):
    hi = x.astype(BF16)
    r1 = x - hi.astype(F32)
    mid = r1.astype(BF16)
    lo = (r1 - mid.astype(F32)).astype(BF16)
    return hi, mid, lo


def _dot3(w01, x):
    hi, mid, lo = _split3(x)
    return _dot(w01, hi) + _dot(w01, mid) + _dot(w01, lo)


def _dot2r(x, w):
    hi = x.astype(BF16)
    mid = (x - hi.astype(F32)).astype(BF16)
    return _dot(hi, w) + _dot(mid, w)


def _sigmoid(x):
    return 1.0 / (1.0 + jnp.exp(-x))


def _silu(x):
    return x * _sigmoid(x)


def _softplus(x):
    return jnp.maximum(x, 0.0) + jnp.log1p(jnp.exp(-jnp.abs(x)))


def _gelu_tanh(x):
    return 0.5 * x * (1.0 + jnp.tanh(math.sqrt(2.0 / math.pi) * (x + 0.044715 * (x * x * x))))


def _rms(x, w):
    return x * lax.rsqrt(jnp.mean(x * x, axis=-1, keepdims=True) + EPS) * w


def _stack4(x):
    return jnp.concatenate([x, x, x, x], axis=0)


def _shift_rows(x, k, prev_rows, seq_len):
    if seq_len is None:
        ext = jnp.concatenate([prev_rows, x], axis=0)
        return pltpu.roll(ext, k, axis=0)[8:]
    ridx = lax.broadcasted_iota(jnp.int32, x.shape, 0)
    keep = (ridx & (seq_len - 1)) >= k
    return jnp.where(keep, pltpu.roll(x, k, axis=0), prev_rows)


def _causal_conv(x, w_ref, prevs, seq_len):
    y = w_ref[4:5, :] + x * w_ref[3:4, :]
    for k in (1, 2, 3):
        prev = prevs if seq_len is None else prevs[k - 1]
        y = y + _shift_rows(x, k, prev, seq_len) * w_ref[3 - k:4 - k, :]
    return y


def _swap_halves(x):
    w = x.shape[1]
    lane = lax.broadcasted_iota(jnp.int32, x.shape, 1)
    return jnp.where((lane & 32) == 0, pltpu.roll(x, w - 32, axis=1), pltpu.roll(x, 32, axis=1))


def _expand_groups(x):
    lane = lax.broadcasted_iota(jnp.int32, x.shape, 1)
    r = pltpu.roll(x, 64, axis=1)
    first = lane < 64
    return jnp.concatenate([jnp.where(first, x, r), jnp.where(first, r, x)], axis=1)


def _head_cols(cum):
    cols = []
    for half in range(2):
        x = cum[:, half * 128:(half + 1) * 128]
        lane = lax.broadcasted_iota(jnp.int32, x.shape, 1)
        r = pltpu.roll(x, 64, axis=1)
        first = lane < 64
        cols.append(jnp.where(first, x, r))
        cols.append(jnp.where(first, r, x))
    return cols


def _mid_cum(cum, n):
    rows, w = cum.shape
    half = n // 2
    if n >= 16:
        blocks = cum.reshape(rows // n, n, w)
        return jnp.broadcast_to(blocks[:, half - 1:half, :], blocks.shape).reshape(rows, w)
    pos = lax.broadcasted_iota(jnp.int32, cum.shape, 0) & (n - 1)
    out = cum
    for p in range(n):
        delta = half - 1 - p
        if delta != 0:
            out = jnp.where(pos == p, pltpu.roll(cum, (-delta) % rows, axis=0), out)
    return out


def _scan_rows(a, u, seq_len):
    rows = a.shape[0]
    ridx = lax.broadcasted_iota(jnp.int32, a.shape, 0)
    pos = ridx if seq_len is None else (ridx & (seq_len - 1))
    limit = rows if seq_len is None else seq_len
    s = 1
    while s < limit:
        keep = pos >= s
        a_s = jnp.where(keep, pltpu.roll(a, s, axis=0), 1.0)
        u_s = jnp.where(keep, pltpu.roll(u, s, axis=0), 0.0)
        u = a * u_s + u
        a = a * a_s
        s *= 2
    return u


def _mixer_kernel(*refs, sample, n_chunks, level_sizes, n_alias):
    it = iter(refs)
    x_ref, cos_ref, sin_ref = next(it), next(it), next(it)
    if sample:
        e_ssd_ref, e_lru_ref, h0_ref = next(it), next(it), next(it)
        s_ssd_ref, s_ret_ref, s_gla_ref = next(it), next(it), next(it)
        for _ in range(n_alias):
            next(it)
    W = {n: next(it) for n in WEIGHT_NAMES}
    K = {n: next(it) for n in (SAMPLE_CONST_NAMES if sample else CONST_NAMES)}
    hmid_ref = next(it)
    if sample:
        xbc_out_ref, lrux_out_ref, hall_ref = next(it), next(it), next(it)
        n_ssd_ref, n_ret_ref, n_gla_ref = next(it), next(it), next(it)
        qst_ref, kst_ref, vst_ref, ost_ref = next(it), next(it), next(it), next(it)
    else:
        tail_ssd_ref, tail_lru_ref, hl_ref = next(it), next(it), next(it)
        S_ssd_ref, S_ret_ref, S_gla_ref = next(it), next(it), next(it)

    C = CHUNK
    n_levels = len(level_sizes)
    vec = W["vec"]
    seq_len = 4 if sample else None

    if not sample:
        @pl.when(pl.program_id(1) == 0)
        def _():
            tail_ssd_ref[...] = jnp.zeros_like(tail_ssd_ref)
            tail_lru_ref[...] = jnp.zeros_like(tail_lru_ref)
            hl_ref[...] = jnp.zeros_like(hl_ref)
            S_ssd_ref[...] = jnp.zeros_like(S_ssd_ref)
            S_ret_ref[...] = jnp.zeros_like(S_ret_ref)
            S_gla_ref[...] = jnp.zeros_like(S_gla_ref)

    x = x_ref[...]
    xn = _rms(x, W["norm_mix"][...]).astype(BF16)
    u = _dot(xn, W["w_in"][...])
    T = u.shape[0]

    ssd_z = u[:, C_Z:C_Z + 256]
    xbc_raw = u[:, C_XBC:C_XBC + 512]
    dt_raw = u[:, C_DT:C_DT + 256]
    ret_q = u[:, C_RET:C_RET + 256]
    ret_k = u[:, C_RET + 256:C_RET + 512]
    ret_v = u[:, C_RET + 512:C_RET + 768]
    ret_g = u[:, C_RET + 768:C_RET + 1024]
    gla_q = u[:, C_GQ:C_GQ + 128]
    gla_k = u[:, C_GK:C_GK + 128]
    gla_v = u[:, C_GV:C_GV + 256]
    gla_r = u[:, C_GR:C_GR + 256]
    lru_y = u[:, C_LY:C_LY + 256]
    lru_x = u[:, C_LX:C_LX + 256]
    misc = u[:, C_MISC:C_MISC + 128]

    if sample:
        prev_ssd = [e_ssd_ref[k] for k in range(3)]
        prev_lru = [e_lru_ref[k] for k in range(3)]
    else:
        prev_ssd = tail_ssd_ref[...]
        prev_lru = tail_lru_ref[...]
    xbc = _silu(_causal_conv(xbc_raw, W["cw_ssd"], prev_ssd, seq_len))
    xs = xbc[:, :256]
    ssd_k = _expand_groups(xbc[:, 256:384])
    ssd_q = _expand_groups(xbc[:, 384:512])
    dt = _softplus(dt_raw + vec[V_DTB:V_DTB + 1, :])
    ssd_la = -dt * jnp.exp(vec[V_ALOG:V_ALOG + 1, :])
    ssd_v = xs * dt

    cos = cos_ref[...]
    sin = sin_ref[...]
    rq = ret_q * cos + _swap_halves(ret_q) * sin
    rk = (ret_k * cos + _swap_halves(ret_k) * sin) * (HEAD_DIM ** -0.5)

    gate_x = _dot(misc.astype(BF16), W["w2"][...]) + W["bg"][0:1, :]
    gla_lg = (jnp.minimum(gate_x, 0.0) - jnp.log1p(jnp.exp(-jnp.abs(gate_x)))) / GLA_GATE_TEMP
    gq = gla_q * (GLA_DK ** -0.5)

    xc = _causal_conv(lru_x, W["cw_lru"], prev_lru, seq_len)
    xcb = xc.astype(BF16)
    lr = _sigmoid(_dot(xcb, W["wa"][...]) + vec[V_LBA:V_LBA + 1, :])
    li = _sigmoid(_dot(xcb, W["wx"][...]) + vec[V_LBX:V_LBX + 1, :])
    l_loga = -LRU_C * lr * _softplus(-vec[V_LLAM:V_LLAM + 1, :])
    l_a = jnp.exp(l_loga)
    l_u = jnp.sqrt(-jnp.tanh(l_loga) * (l_a * l_a + 1.0)) * (li * xc)
    if sample:
        l_u = l_u + l_a * h0_ref[...]
    else:
        first = lax.broadcasted_iota(jnp.int32, l_u.shape, 0) == 0
        l_u = l_u + jnp.where(first, l_a * hl_ref[7:8, :], 0.0)
    l_h = _scan_rows(l_a, l_u, seq_len)
    y_lru = l_h * _gelu_tanh(lru_y)

    cmask = K["cmask"][...] > 0.0
    sm256 = K["sm256"][...]
    sm128 = K["sm128"][...]

    def stack_kv(a, mask):
        return _stack4(a.astype(BF16)) * mask

    def cum_and_rest(la):
        if sample:
            e = _dot3(K["tu"][...], la)
            return e[:C], e[C:]
        cum = _dot3(K["tu"][:C, :], la)
        return cum, cum[C - 1:C, :] - cum

    def sample_states(qt, kt, v, a_last, s_ref, n_ref, dk, rm):
        rep = K["rep"][...]
        qst_ref[:, :dk] = (_dot(rep, qt.astype(BF16)).astype(BF16) * rm)
        kst_ref[:, :dk] = (_dot(rep, kt.astype(BF16)).astype(BF16) * rm)
        for h in range(N_HEADS):
            vst_ref[pl.ds(h, C, stride=N_HEADS), :] = v[:, h * HEAD_DIM:(h + 1) * HEAD_DIM]
        a_t = jnp.concatenate([a_last[:, i * 128:(i + 1) * 128].T for i in range(dk // 128)], axis=0)
        lane = lax.broadcasted_iota(jnp.int32, a_t.shape, 1)

        def body(j, carry):
            r0 = pl.multiple_of(j * 16, 16)
            s0 = s_ref[j]
            ost_ref[pl.ds(r0, 16), :] = _dot(qst_ref[pl.ds(r0, 16), :dk], s0.astype(BF16))
            ds = _dot_tn(kst_ref[pl.ds(r0, 16), :dk], vst_ref[pl.ds(r0, 16), :].astype(BF16))
            col = jnp.sum(jnp.where(lane == 4 * j + 3, a_t, 0.0), axis=1, keepdims=True)
            n_ref[j] = col * s0 + ds
            return carry

        lax.fori_loop(0, SAMPLE_SEQS, body, 0, unroll=4)
        return jnp.concatenate(
            [ost_ref[pl.ds(h, C, stride=N_HEADS), :] for h in range(N_HEADS)], axis=1)

    o_ssd, o_ret, o_gla = [], [], []
    for c in range(n_chunks):
        sl = slice(c * C, (c + 1) * C)

        cum, rem = cum_and_rest(ssd_la[sl])
        q, k, v = ssd_q[sl], ssd_k[sl], ssd_v[sl]
        dec = jnp.concatenate(
            [jnp.exp(jnp.where(cmask, col - col.T, -jnp.inf)) for col in _head_cols(cum)], axis=1)
        p = _dot_nt(q.astype(BF16), stack_kv(k, sm256)) * dec
        o = _dot(p.astype(BF16), stack_kv(v, sm256))
        a_cum = jnp.exp(cum)
        qt = q * a_cum
        kt = k * jnp.exp(rem)
        if sample:
            o = o + sample_states(qt, kt, v, a_cum, s_ssd_ref, n_ssd_ref, 256, K["rm256"][...])
        else:
            s_old = S_ssd_ref[...]
            o = o + _dot(qt.astype(BF16), s_old.astype(BF16))
            ds = _dot_tn(kt.astype(BF16), v.astype(BF16)) * K["bdm"][...]
            S_ssd_ref[...] = s_old * a_cum[C - 1:C, :] + ds
        o_ssd.append(o)

        q, k, v = rq[sl], rk[sl], ret_v[sl]
        p = _dot_nt(q.astype(BF16), stack_kv(k, sm256)) * K["dret"][...]
        o = _dot(p.astype(BF16), stack_kv(v, sm256))
        qt = q * K["retq"][...]
        kt = k * K["retk"][...]
        if sample:
            o = o + sample_states(qt, kt, v, K["retq"][...], s_ret_ref, n_ret_ref, 256, K["rm256"][...])
        else:
            s_old = S_ret_ref[...]
            o = o + _dot(qt.astype(BF16), s_old.astype(BF16))
            ds = _dot_tn(kt.astype(BF16), v.astype(BF16)) * K["bdm"][...]
            S_ret_ref[...] = s_old * K["rets"][0:1, :] + ds
        o_ret.append(o)

        cum, rem = cum_and_rest(gla_lg[sl])
        q, k, v = gq[sl], gla_k[sl], gla_v[sl]
        p = _dot_nt(q.astype(BF16), stack_kv(k, sm128)) * K["gmask"][n_levels]
        for lvl, n in enumerate(level_sizes):
            mid = _mid_cum(cum, n)
            up = K["gup"][lvl] > 0.0
            f = jnp.exp(jnp.where(up, cum - mid, mid - cum))
            ql = jnp.where(up, q * f, 0.0)
            kl = jnp.where(up, 0.0, k * f)
            p = p + _dot_nt(ql.astype(BF16), stack_kv(kl, sm128)) * K["gmask"][lvl]
        o = _dot(p.astype(BF16), stack_kv(v, sm256))
        a_cum = jnp.exp(cum)
        qt = q * a_cum
        kt = k * jnp.exp(rem)
        if sample:
            o = o + sample_states(qt, kt, v, a_cum, s_gla_ref, n_gla_ref, 128, K["rm128"][...])
        else:
            s_old = S_gla_ref[...]
            o = o + _dot(qt.astype(BF16), s_old.astype(BF16))
            ds = _dot_tn(kt.astype(BF16), v.astype(BF16)) * K["bdg"][...]
            a_col = a_cum.T[:, C - 1:C]
            S_gla_ref[...] = s_old * a_col + ds
        o_gla.append(o)

    cat = (lambda xs_: xs_[0]) if n_chunks == 1 else (lambda xs_: jnp.concatenate(xs_, axis=0))
    o_ssd, o_ret, o_gla = cat(o_ssd), cat(o_ret), cat(o_gla)

    mavg = K["mavg"][...]
    y_ssd = (o_ssd + xs * vec[V_SSDD:V_SSDD + 1, :]) * _silu(ssd_z)
    y_ssd = _rms(y_ssd, vec[V_SSDN:V_SSDN + 1, :])
    mu = _dot2r(o_ret, mavg)
    d = o_ret - mu
    var = _dot2r(d * d, mavg)
    y_ret = d * lax.rsqrt(var + EPS) * vec[V_RETN:V_RETN + 1, :] * _silu(ret_g)
    ms = _dot2r(o_gla * o_gla, mavg)
    y_gla = o_gla * lax.rsqrt(ms + EPS) * vec[V_GLAN:V_GLAN + 1, :] * _silu(gla_r)

    ymix = jnp.concatenate([y_ssd, y_ret, y_gla, y_lru], axis=1).astype(BF16)
    hmid_ref[...] = x + _dot(ymix, W["w_out"][...])

    if sample:
        xbc_out_ref[...] = xbc_raw
        lrux_out_ref[...] = lru_x
        hall_ref[...] = l_h
    else:
        tail_ssd_ref[...] = xbc_raw[T - 8:, :]
        tail_lru_ref[...] = lru_x[T - 8:, :]
        hl_ref[...] = l_h[T - 8:, :]


def _ffn_kernel(x_ref, nf_ref, wg_ref, wu_ref, wd_ref, nfin_ref, o_ref, *, final_norm):
    x = x_ref[...]
    hn = _rms(x, nf_ref[...]).astype(BF16)
    g = _dot(hn, wg_ref[...])
    up = _dot(hn, wu_ref[...])
    act = (_silu(g) * up).astype(BF16)
    y = x + _dot(act, wd_ref[...])
    if final_norm:
        y = _rms(y, nfin_ref[...])
    o_ref[...] = y


def _layer_spec(a, layer, grid_rank):
    idx = (layer,) + (0,) * (a.ndim - 1)
    imap = (lambda i: idx) if grid_rank == 1 else (lambda b, t: idx)
    return pl.BlockSpec((None,) + a.shape[1:], imap, pipeline_mode=pl.Buffered(1))


def _full_spec(a, grid_rank):
    zeros = (0,) * a.ndim
    return pl.BlockSpec(a.shape, (lambda i: zeros) if grid_rank == 1 else (lambda b, t: zeros))


def _ffn(h2d, layer, fw, norm_final, final_norm, tm):
    rows = h2d.shape[0]
    weights = [fw[n] for n in ("norm_ffn", "wg", "wu", "wd")]
    return pl.pallas_call(
        functools.partial(_ffn_kernel, final_norm=final_norm),
        out_shape=jax.ShapeDtypeStruct(h2d.shape, F32),
        grid=(rows // tm,),
        in_specs=[pl.BlockSpec((tm, D_MODEL), lambda i: (i, 0))]
                 + [_layer_spec(a, layer, 1) for a in weights] + [_full_spec(norm_final, 1)],
        out_specs=pl.BlockSpec((tm, D_MODEL), lambda i: (i, 0)),
        compiler_params=pltpu.CompilerParams(
            dimension_semantics=("arbitrary",), vmem_limit_bytes=VMEM_LIMIT_BYTES),
        name="ffn",
    )(h2d, *weights, norm_final)


def _repack_kernel(w_ref, o_ref):
    rows = w_ref.shape[0]
    cp = lambda dst, src, n: o_ref.__setitem__(
        (slice(None), slice(dst, dst + n)), w_ref[:, src:src + n].astype(BF16))
    cp(C_Z, 0, 768)
    dt = [jnp.broadcast_to(w_ref[:, S_DT + h:S_DT + h + 1], (rows, HEAD_DIM)) for h in range(N_HEADS)]
    o_ref[:, C_DT:C_DT + 256] = jnp.concatenate(dt, axis=1).astype(BF16)
    cp(C_RET, S_RET, 1024)
    cp(C_GQ, S_GLA, 512)
    cp(C_GR, S_GR, 256)
    cp(C_LY, S_LRU, 512)
    base = (S_GLR // 128) * 128
    win = pltpu.roll(w_ref[:, base:base + 128], 128 - (S_GLR - base), axis=1)
    lane = lax.broadcasted_iota(jnp.int32, win.shape, 1)
    o_ref[:, C_MISC:C_MISC + 128] = jnp.where(lane < GLA_GATE_RANK, win, 0.0).astype(BF16)


def _repack_w_in(w_in):
    depth = w_in.shape[0]
    rb = 256
    return pl.pallas_call(
        _repack_kernel,
        out_shape=jax.ShapeDtypeStruct((depth, D_MODEL, IN_PACKED), BF16),
        grid=(depth, D_MODEL // rb),
        in_specs=[pl.BlockSpec((None, rb, IN_DIM), lambda l, r: (l, r, 0))],
        out_specs=pl.BlockSpec((None, rb, IN_PACKED), lambda l, r: (l, r, 0)),
        compiler_params=pltpu.CompilerParams(dimension_semantics=("arbitrary", "arbitrary")),
        name="repack_w_in",
    )(w_in)


def _pack_weights(p):
    depth = p["w_in"].shape[0]
    pad_rows = lambda a, n: jnp.concatenate(
        [a, jnp.zeros((depth, n - a.shape[1], a.shape[2]), F32)], axis=1)
    rep = lambda a: jnp.repeat(a, HEAD_DIM, axis=1)
    vec = jnp.stack([rep(p["ssd_dt_bias"]), rep(p["ssd_a_log"]), rep(p["ssd_d"]),
                     p["ssd_norm"], p["ret_norm"], p["gla_norm"],
                     p["lru_b_a"], p["lru_b_x"], p["lru_lambda"]], axis=1)
    eye = jnp.eye(N_HEADS, dtype=F32)
    bd = lambda a: jnp.einsum("lhij,hg->lhigj", a, eye).reshape(depth, GROUP_WIDTH, GROUP_WIDTH).astype(BF16)
    return dict(
        norm_mix=p["norm_mix"][:, None, :], w_in=_repack_w_in(p["w_in"]),
        cw_ssd=pad_rows(jnp.concatenate([p["ssd_conv_w"], p["ssd_conv_b"][:, None, :]], axis=1), 8),
        cw_lru=pad_rows(jnp.concatenate([p["lru_conv_w"], p["lru_conv_b"][:, None, :]], axis=1), 8),
        vec=pad_rows(vec, 16), bg=pad_rows(p["gla_b_gate"][:, None, :], 8),
        w2=pad_rows(p["gla_w_gate2"], 128).astype(BF16),
        wa=bd(p["lru_w_a"]), wx=bd(p["lru_w_x"]), w_out=p["w_out"].astype(BF16))


def _rope_tables(pos):
    half = HEAD_DIM // 2
    inv = ROPE_BASE ** (-jnp.arange(half, dtype=F32) / half)
    ang = pos.astype(F32)[:, None] * inv[None, :]
    cos, sin = jnp.cos(ang), jnp.sin(ang)
    cos = jnp.tile(jnp.concatenate([cos, cos], axis=1), (1, N_HEADS))
    sin = jnp.tile(jnp.concatenate([-sin, sin], axis=1), (1, N_HEADS))
    return cos, sin


def _prompt_mixer(h, cos, sin, layer, wts, consts, tq):
    b, l, _ = h.shape
    weights = [wts[n] for n in WEIGHT_NAMES]
    cvals = [consts[n] for n in CONST_NAMES]
    row = lambda w: pl.BlockSpec((None, tq, w), lambda bi, ti: (bi, ti, 0))
    per_seq = lambda r, w: pl.BlockSpec((None, r, w), lambda bi, ti: (bi, 0, 0))
    out_shapes = (jax.ShapeDtypeStruct((b, l, D_MODEL), F32),
                  jax.ShapeDtypeStruct((b, 8, SSD_CONV_DIM), F32),
                  jax.ShapeDtypeStruct((b, 8, GROUP_WIDTH), F32),
                  jax.ShapeDtypeStruct((b, 8, GROUP_WIDTH), F32),
                  jax.ShapeDtypeStruct((b, 256, 256), F32),
                  jax.ShapeDtypeStruct((b, 256, 256), F32),
                  jax.ShapeDtypeStruct((b, 128, 256), F32))
    return pl.pallas_call(
        functools.partial(_mixer_kernel, sample=False, n_chunks=tq // CHUNK,
                          level_sizes=_level_sizes(CHUNK), n_alias=0),
        out_shape=out_shapes,
        grid=(b, l // tq),
        in_specs=[row(D_MODEL),
                  pl.BlockSpec((tq, GROUP_WIDTH), lambda bi, ti: (ti, 0)),
                  pl.BlockSpec((tq, GROUP_WIDTH), lambda bi, ti: (ti, 0))]
                 + [_layer_spec(a, layer, 2) for a in weights] + [_full_spec(a, 2) for a in cvals],
        out_specs=(row(D_MODEL), per_seq(8, SSD_CONV_DIM), per_seq(8, GROUP_WIDTH),
                   per_seq(8, GROUP_WIDTH), per_seq(256, 256), per_seq(256, 256), per_seq(128, 256)),
        compiler_params=pltpu.CompilerParams(
            dimension_semantics=("arbitrary", "arbitrary"), vmem_limit_bytes=VMEM_LIMIT_BYTES),
        name="prompt_mixer",
    )(h, cos, sin, *weights, *cvals)


def _sample_mixer(h2d, cos, sin, e_ssd, e_lru, h0, states, prev_new, layer, wts, consts):
    rows = h2d.shape[0]
    c = CHUNK
    weights = [wts[n] for n in WEIGHT_NAMES]
    cvals = [consts[n] for n in SAMPLE_CONST_NAMES]
    aliased = list(prev_new) if prev_new is not None else []
    row = lambda w: pl.BlockSpec((c, w), lambda i: (i, 0))
    st = lambda dk: pl.BlockSpec((None, SAMPLE_SEQS, dk, HEAD_DIM), lambda i: (layer, i, 0, 0))
    st_out = lambda dk: pl.BlockSpec((None, SAMPLE_SEQS, dk, HEAD_DIM), lambda i: (layer, i, 0, 0),
                                     pipeline_mode=pl.Buffered(1))
    out_shapes = (jax.ShapeDtypeStruct((rows, D_MODEL), F32),
                  jax.ShapeDtypeStruct((rows, SSD_CONV_DIM), F32),
                  jax.ShapeDtypeStruct((rows, GROUP_WIDTH), F32),
                  jax.ShapeDtypeStruct((rows, GROUP_WIDTH), F32)) + tuple(
                      jax.ShapeDtypeStruct(s.shape, F32) for s in states)
    n_lead = 9
    return pl.pallas_call(
        functools.partial(_mixer_kernel, sample=True, n_chunks=1,
                          level_sizes=_level_sizes(4), n_alias=len(aliased)),
        out_shape=out_shapes,
        grid=(rows // c,),
        in_specs=[row(D_MODEL), row(GROUP_WIDTH), row(GROUP_WIDTH),
                  pl.BlockSpec((3, c, SSD_CONV_DIM), lambda i: (0, i, 0)),
                  pl.BlockSpec((3, c, GROUP_WIDTH), lambda i: (0, i, 0)),
                  row(GROUP_WIDTH), st(256), st(256), st(128)]
                 + [pl.BlockSpec(memory_space=pl.ANY) for _ in aliased]
                 + [_layer_spec(a, layer, 1) for a in weights] + [_full_spec(a, 1) for a in cvals],
        out_specs=(row(D_MODEL), row(SSD_CONV_DIM), row(GROUP_WIDTH), row(GROUP_WIDTH),
                   st_out(256), st_out(256), st_out(128)),
        scratch_shapes=[pltpu.VMEM((N_HEADS * c, 256), BF16), pltpu.VMEM((N_HEADS * c, 256), BF16),
                        pltpu.VMEM((N_HEADS * c, HEAD_DIM), F32), pltpu.VMEM((N_HEADS * c, HEAD_DIM), F32)],
        input_output_aliases={n_lead + k: 4 + k for k in range(len(aliased))},
        compiler_params=pltpu.CompilerParams(
            dimension_semantics=("arbitrary",), vmem_limit_bytes=VMEM_LIMIT_BYTES),
        name="sample_mixer",
    )(h2d, cos, sin, e_ssd, e_lru, h0, *states, *aliased, *weights, *cvals)


def _incoming_rows(state):
    b, _, ch = state.shape
    outs = []
    for k in (1, 2, 3):
        rows = [state[:, 3 + t - k, :] if t < k else jnp.zeros((b, ch), F32) for t in range(4)]
        outs.append(jnp.stack(rows, axis=1).reshape(b * 4, ch))
    return jnp.stack(outs, axis=0)


def _diag_blocks(s, dk):
    return jnp.stack([s[:, h * dk:(h + 1) * dk, h * HEAD_DIM:(h + 1) * HEAD_DIM]
                      for h in range(N_HEADS)], axis=1)


def kernel(x_prompt, x_sample, state_ssd_conv, state_ssd, state_ret, state_gla, state_lru_conv, state_lru,
           norm_mix, w_in, ssd_conv_w, ssd_conv_b, ssd_dt_bias, ssd_a_log, ssd_d, ssd_norm, ret_norm,
           gla_w_gate2, gla_b_gate, gla_norm, lru_conv_w, lru_conv_b, lru_w_a, lru_b_a, lru_w_x, lru_b_x,
           lru_lambda, w_out, norm_ffn, w_gate, w_up, w_down, norm_final):
    params = dict(norm_mix=norm_mix, w_in=w_in, ssd_conv_w=ssd_conv_w, ssd_conv_b=ssd_conv_b,
                  ssd_dt_bias=ssd_dt_bias, ssd_a_log=ssd_a_log, ssd_d=ssd_d, ssd_norm=ssd_norm,
                  ret_norm=ret_norm, gla_w_gate2=gla_w_gate2, gla_b_gate=gla_b_gate, gla_norm=gla_norm,
                  lru_conv_w=lru_conv_w, lru_conv_b=lru_conv_b, lru_w_a=lru_w_a, lru_b_a=lru_b_a,
                  lru_w_x=lru_w_x, lru_b_x=lru_b_x, lru_lambda=lru_lambda, w_out=w_out)
    depth = w_in.shape[0]
    bp, lp, _ = x_prompt.shape
    bs, ls, _ = x_sample.shape
    assert ls == 4 and (bs * ls) % CHUNK == 0 and lp % CHUNK == 0
    tq = 512 if lp % 512 == 0 else CHUNK
    tm_p = 512 if (bp * lp) % 512 == 0 else CHUNK
    tm_s = 256 if (bs * ls) % 256 == 0 else CHUNK

    wts = _pack_weights(params)
    fw = dict(norm_ffn=norm_ffn[:, None, :], wg=w_gate.astype(BF16), wu=w_up.astype(BF16),
              wd=w_down.astype(BF16))
    consts_p = _const_arrays(CHUNK)
    consts_s = _const_arrays(4)
    cos_p, sin_p = _rope_tables(jnp.arange(lp, dtype=jnp.int32))
    cos_s, sin_s = _rope_tables(PAST_LEN + jnp.arange(ls, dtype=jnp.int32))
    cos_s, sin_s = jnp.tile(cos_s, (bs, 1)), jnp.tile(sin_s, (bs, 1))
    nfin = norm_final[None]
    states_s = (state_ssd.reshape(depth, bs, 256, HEAD_DIM), state_ret.reshape(depth, bs, 256, HEAD_DIM),
                state_gla.reshape(depth, bs, 128, HEAD_DIM))

    hp = x_prompt
    hs = x_sample.reshape(bs * ls, D_MODEL)
    new_p = [[] for _ in range(6)]
    new_s = [[] for _ in range(3)]
    new_states_s = None
    for i in range(depth):
        last = i == depth - 1

        hmid, tail_ssd, tail_lru, hl, s_ssd, s_ret, s_gla = _prompt_mixer(
            hp, cos_p, sin_p, i, wts, consts_p, tq)
        hp = _ffn(hmid.reshape(bp * lp, D_MODEL), i, fw, nfin, last, tm_p).reshape(bp, lp, D_MODEL)
        for j, a in enumerate((tail_ssd[:, 5:], _diag_blocks(s_ssd, 64), _diag_blocks(s_ret, 64),
                               _diag_blocks(s_gla, 32), tail_lru[:, 5:], hl[:, 7])):
            new_p[j].append(a)

        h0 = jnp.concatenate([state_lru[i][:, None, :], jnp.zeros((bs, 3, GROUP_WIDTH), F32)],
                             axis=1).reshape(bs * ls, GROUP_WIDTH)
        hmid, xbc_raw, lrux_raw, hall, *new_states_s = _sample_mixer(
            hs, cos_s, sin_s, _incoming_rows(state_ssd_conv[i]), _incoming_rows(state_lru_conv[i]), h0,
            states_s, new_states_s, i, wts, consts_s)
        hs = _ffn(hmid, i, fw, nfin, last, tm_s)
        for j, a in enumerate((xbc_raw.reshape(bs, ls, -1)[:, 1:], lrux_raw.reshape(bs, ls, -1)[:, 1:],
                               hall.reshape(bs, ls, -1)[:, 3])):
            new_s[j].append(a)

    outs_p = [jnp.stack(a, axis=0) for a in new_p]
    s_conv, s_lru_conv, s_lru = [jnp.stack(a, axis=0) for a in new_s]
    n_ssd, n_ret, n_gla = new_states_s
    return (hp, hs.reshape(bs, ls, D_MODEL), *outs_p,
            s_conv, n_ssd.reshape(depth, bs, N_HEADS, HEAD_DIM, HEAD_DIM),
            n_ret.reshape(depth, bs, N_HEADS, HEAD_DIM, HEAD_DIM),
            n_gla.reshape(depth, bs, N_HEADS, GLA_DK, HEAD_DIM), s_lru_conv, s_lru)
```

```python
import functools
import math

import numpy as np
import jax
import jax.numpy as jnp
from jax import lax
from jax.experimental import pallas as pl
from jax.experimental.pallas import tpu as pltpu

F32 = jnp.float32
BF16 = jnp.bfloat16

D_MODEL = 1024
HEAD_DIM = 64
N_HEADS = 4
GROUP_WIDTH = 256
EPS = 1e-6
SSD_CONV_DIM = 512
GLA_KEY_WIDTH = 128
GLA_DK = 32
GLA_GATE_RANK = 16
GLA_GATE_TEMP = 16.0
LRU_C = 8.0
ROPE_BASE = 10000.0
PAST_LEN = 16384
IN_DIM = 3092

C_Z, C_XBC, C_DT, C_RET, C_GQ, C_GK, C_GV, C_GR, C_LY, C_LX, C_MISC, IN_PACKED = (
    0, 256, 768, 1024, 2048, 2176, 2304, 2560, 2816, 3072, 3328, 3456)
S_DT, S_RET, S_GLA, S_GLR, S_GR, S_LRU = 768, 772, 1796, 2308, 2324, 2580

CHUNK = 128
SAMPLE_B = 128
SAMPLE_T = 4
VMEM_LIMIT_BYTES = 56 * 1024 * 1024


def _level_sizes(seg):
    sizes = []
    n = seg
    while n >= 2:
        sizes.append(n)
        n //= 2
    return tuple(sizes)


@functools.lru_cache(maxsize=None)
def _chunk_constants():
    c = CHUNK
    i = np.arange(c)[:, None]
    j = np.arange(c)[None, :]
    up_levels, m_levels = [], []
    for n in _level_sizes(c):
        up_levels.append(np.broadcast_to((i % n) >= n // 2, (c, GLA_KEY_WIDTH)))
        m_levels.append(np.tile((i // n) == (j // n), (1, N_HEADS)))
    m_levels.append(np.tile(i == j, (1, N_HEADS)))
    rows = np.arange(N_HEADS * c)[:, None]
    hh = np.arange(GROUP_WIDTH) // HEAD_DIM
    return dict(
        tri=(j <= i).astype(np.float32),
        gmask=np.stack(m_levels).astype(np.float32),
        gup=np.stack(up_levels).astype(np.float32),
        sm256=((rows // c) == hh[None, :]).astype(np.float32),
        sm128=((rows // c) == (np.arange(GLA_KEY_WIDTH)[None, :] // GLA_DK)).astype(np.float32),
        bdm=(hh[:, None] == hh[None, :]).astype(np.float32),
        bdg=((np.arange(GLA_KEY_WIDTH) // GLA_DK)[:, None] == hh[None, :]).astype(np.float32))


def _ret_log_gamma():
    return jnp.log(1.0 - 2.0 ** (-5.0 - jnp.arange(N_HEADS, dtype=F32)))


def _const_arrays():
    k = _chunk_constants()
    c = CHUNK
    log_gamma = _ret_log_gamma()
    lg256 = jnp.repeat(log_gamma, HEAD_DIM)[None, :]
    pos = jnp.arange(c, dtype=F32)[:, None]
    dpos = pos - pos.T
    tri = jnp.asarray(k["tri"])
    dret = jnp.concatenate(
        [jnp.where(tri > 0, jnp.exp(dpos * log_gamma[h]), 0.0) for h in range(N_HEADS)], axis=1)
    return dict(
        tri=jnp.asarray(k["tri"], BF16),
        cmask=tri, gmask=jnp.asarray(k["gmask"]), gup=jnp.asarray(k["gup"]),
        retq=jnp.exp((pos + 1.0) * lg256), retk=jnp.exp((c - 1.0 - pos) * lg256),
        rets=jnp.broadcast_to(jnp.exp(float(c) * lg256), (8, GROUP_WIDTH)), dret=dret,
        sm256=jnp.asarray(k["sm256"], BF16), sm128=jnp.asarray(k["sm128"], BF16),
        bdm=jnp.asarray(k["bdm"]), bdg=jnp.asarray(k["bdg"]),
        mavg=jnp.asarray(k["bdm"] / HEAD_DIM, BF16))


CONST_NAMES = ("tri", "cmask", "gmask", "gup", "retq", "retk", "rets", "dret",
               "sm256", "sm128", "bdm", "bdg", "mavg")
WEIGHT_NAMES = ("norm_mix", "w_in", "cw_ssd", "cw_lru", "vec", "bg", "w2", "wa", "wx", "w_out")

V_DTB, V_ALOG, V_SSDD, V_SSDN, V_RETN, V_GLAN, V_LBA, V_LBX, V_LLAM = range(9)


def _dot(a, b):
    return jnp.dot(a, b, preferred_element_type=F32)


def _dot_nt(a, b):
    return lax.dot_general(a, b, (((1,), (1,)), ((), ())), preferred_element_type=F32)


def _dot_tn(a, b):
    return lax.dot_general(a, b, (((0,), (0,)), ((), ())), preferred_element_type=F32)


def _split3(x):
    hi = x.astype(BF16)
    r1 = x - hi.astype(F32)
    mid = r1.astype(BF16)
    lo = (r1 - mid.astype(F32)).astype(BF16)
    return hi, mid, lo


def _dot3(w01, x):
    hi, mid, lo = _split3(x)
    return _dot(w01, hi) + _dot(w01, mid) + _dot(w01, lo)


def _sigmoid(x):
    return 1.0 / (1.0 + jnp.exp(-x))


def _silu(x):
    return x * _sigmoid(x)


def _log1pexp_neg_abs(x):
    return jnp.log(1.0 + jnp.exp(-jnp.abs(x)))


def _softplus(x):
    return jnp.maximum(x, 0.0) + _log1pexp_neg_abs(x)


def _gelu_tanh(x):
    return 0.5 * x * (1.0 + jnp.tanh(math.sqrt(2.0 / math.pi) * (x + 0.044715 * (x * x * x))))


def _rms(x, w):
    return x * lax.rsqrt(jnp.mean(x * x, axis=-1, keepdims=True) + EPS) * w


def _stack4(x):
    return jnp.concatenate([x, x, x, x], axis=0)


def _causal_conv_carried(x, w_ref, buf_ref):
    rows = x.shape[0]
    buf_ref[8:8 + rows, :] = x
    y = w_ref[4:5, :] + x * w_ref[3:4, :]
    for k in (1, 2, 3):
        y = y + buf_ref[8 - k:8 - k + rows, :] * w_ref[3 - k:4 - k, :]
    buf_ref[0:8, :] = x[rows - 8:, :]
    return y


def _swap_halves(x):
    w = x.shape[1]
    lane = lax.broadcasted_iota(jnp.int32, x.shape, 1)
    return jnp.where((lane & 32) == 0, pltpu.roll(x, w - 32, axis=1), pltpu.roll(x, 32, axis=1))


def _expand_groups(x):
    lane = lax.broadcasted_iota(jnp.int32, x.shape, 1)
    r = pltpu.roll(x, 64, axis=1)
    first = lane < 64
    return jnp.concatenate([jnp.where(first, x, r), jnp.where(first, r, x)], axis=1)


def _head_cols(cum):
    cols = []
    for half in range(2):
        x = cum[:, half * 128:(half + 1) * 128]
        lane = lax.broadcasted_iota(jnp.int32, x.shape, 1)
        r = pltpu.roll(x, 64, axis=1)
        first = lane < 64
        cols.append(jnp.where(first, x, r))
        cols.append(jnp.where(first, r, x))
    return cols


def _mid_cum(cum, n):
    rows, w = cum.shape
    half = n // 2
    if n >= 16:
        blocks = cum.reshape(rows // n, n, w)
        return jnp.broadcast_to(blocks[:, half - 1:half, :], blocks.shape).reshape(rows, w)
    pos = lax.broadcasted_iota(jnp.int32, cum.shape, 0) & (n - 1)
    out = cum
    for p in range(n):
        delta = half - 1 - p
        if delta != 0:
            out = jnp.where(pos == p, pltpu.roll(cum, (-delta) % rows, axis=0), out)
    return out


def _scan_rows(a, u):
    rows = a.shape[0]
    pos = lax.broadcasted_iota(jnp.int32, a.shape, 0)
    s = 1
    while s < rows:
        keep = pos >= s
        a_s = jnp.where(keep, pltpu.roll(a, s, axis=0), 1.0)
        u_s = jnp.where(keep, pltpu.roll(u, s, axis=0), 0.0)
        u = a * u_s + u
        a = a * a_s
        s *= 2
    return u


def _ffn_math(x, nf_ref, wg_ref, wu_ref, wd_ref, nfin_ref, final_norm):
    hn = _rms(x, nf_ref[...]).astype(BF16)
    g = _dot(hn, wg_ref[...])
    up = _dot(hn, wu_ref[...])
    act = (_silu(g) * up).astype(BF16)
    y = x + _dot(act, wd_ref[...])
    return _rms(y, nfin_ref[...]) if final_norm else y


def _ffn_kernel(x_ref, nf_ref, wg_ref, wu_ref, wd_ref, nfin_ref, o_ref, *, final_norm):
    o_ref[...] = _ffn_math(x_ref[...], nf_ref, wg_ref, wu_ref, wd_ref, nfin_ref, final_norm)


def _prep(cols, conv_ssd_fn, conv_lru_fn, cos, sin, W):
    vec = W["vec"]
    xbc_raw, lru_x = cols(C_XBC, 512), cols(C_LX, 256)
    xbc = _silu(conv_ssd_fn(xbc_raw))
    xs = xbc[:, :256]
    dt = _softplus(cols(C_DT, 256) + vec[V_DTB:V_DTB + 1, :])
    ret_q, ret_k = cols(C_RET, 256), cols(C_RET + 256, 256)
    gate_x = _dot(cols(C_MISC, 128).astype(BF16), W["w2"][...]) + W["bg"][0:1, :]
    xc = conv_lru_fn(lru_x)
    xcb = xc.astype(BF16)
    lr = _sigmoid(_dot(xcb, W["wa"][...]) + vec[V_LBA:V_LBA + 1, :])
    li = _sigmoid(_dot(xcb, W["wx"][...]) + vec[V_LBX:V_LBX + 1, :])
    l_loga = -LRU_C * lr * _softplus(-vec[V_LLAM:V_LLAM + 1, :])
    l_a = jnp.exp(l_loga)
    return dict(
        xbc_raw=xbc_raw, lru_x=lru_x, xs=xs,
        ssd_k=_expand_groups(xbc[:, 256:384]), ssd_q=_expand_groups(xbc[:, 384:512]),
        ssd_la=-dt * jnp.exp(vec[V_ALOG:V_ALOG + 1, :]),
        ssd_v=xs * dt,
        rq=ret_q * cos + _swap_halves(ret_q) * sin,
        rk=(ret_k * cos + _swap_halves(ret_k) * sin) * (HEAD_DIM ** -0.5),
        gla_lg=(jnp.minimum(gate_x, 0.0) - _log1pexp_neg_abs(gate_x)) / GLA_GATE_TEMP,
        gq=cols(C_GQ, 128) * (GLA_DK ** -0.5),
        l_a=l_a,
        l_u=jnp.sqrt(-jnp.tanh(l_loga) * (l_a * l_a + 1.0)) * (li * xc))


def _mix_out(x, o_ssd, o_ret, o_gla, y_lru, xs, g_ssd, g_ret, g_gla, W, mavg):
    vec = W["vec"]
    y_ssd = _rms((o_ssd + xs * vec[V_SSDD:V_SSDD + 1, :]) * g_ssd, vec[V_SSDN:V_SSDN + 1, :])
    mu = _dot(o_ret.astype(BF16), mavg)
    d = o_ret - mu
    var = _dot((d * d).astype(BF16), mavg)
    y_ret = d * lax.rsqrt(var + EPS) * vec[V_RETN:V_RETN + 1, :] * g_ret
    ms = _dot((o_gla * o_gla).astype(BF16), mavg)
    y_gla = o_gla * lax.rsqrt(ms + EPS) * vec[V_GLAN:V_GLAN + 1, :] * g_gla
    ymix = jnp.concatenate([y_ssd, y_ret, y_gla, y_lru], axis=1).astype(BF16)
    return x + _dot(ymix, W["w_out"][...])


def _mixer_kernel(*refs, n_chunks, level_sizes):
    it = iter(refs)
    x_ref, cos_ref, sin_ref = next(it), next(it), next(it)
    W = {n: next(it) for n in WEIGHT_NAMES}
    K = {n: next(it) for n in CONST_NAMES}
    hmid_ref = next(it)
    tail_ssd_ref, tail_lru_ref, hl_ref = next(it), next(it), next(it)
    S_ssd_ref, S_ret_ref, S_gla_ref = next(it), next(it), next(it)
    cbuf_ssd_ref, cbuf_lru_ref = next(it), next(it)

    C = CHUNK
    n_levels = len(level_sizes)

    @pl.when(pl.program_id(1) == 0)
    def _():
        cbuf_ssd_ref[0:8, :] = jnp.zeros((8, SSD_CONV_DIM), F32)
        cbuf_lru_ref[0:8, :] = jnp.zeros((8, GROUP_WIDTH), F32)
        hl_ref[...] = jnp.zeros_like(hl_ref)
        S_ssd_ref[...] = jnp.zeros_like(S_ssd_ref)
        S_ret_ref[...] = jnp.zeros_like(S_ret_ref)
        S_gla_ref[...] = jnp.zeros_like(S_gla_ref)

    x = x_ref[...]
    u = _dot(_rms(x, W["norm_mix"][...]).astype(BF16), W["w_in"][...])
    T = u.shape[0]
    cols = lambda a, w: u[:, a:a + w]

    pr = _prep(cols, lambda v: _causal_conv_carried(v, W["cw_ssd"], cbuf_ssd_ref),
               lambda v: _causal_conv_carried(v, W["cw_lru"], cbuf_lru_ref),
               cos_ref[...], sin_ref[...], W)
    ret_v, gla_k, gla_v = cols(C_RET + 512, 256), cols(C_GK, 128), cols(C_GV, 256)
    first = lax.broadcasted_iota(jnp.int32, (T, GROUP_WIDTH), 0) == 0
    l_h = _scan_rows(pr["l_a"], pr["l_u"] + jnp.where(first, pr["l_a"] * hl_ref[7:8, :], 0.0))
    y_lru = l_h * _gelu_tanh(cols(C_LY, 256))

    cmask = K["cmask"][...] > 0.0
    sm256 = K["sm256"][...]
    sm128 = K["sm128"][...]

    def stack_kv(a, mask):
        return _stack4(a.astype(BF16)) * mask

    def cum_and_rest(la):
        cum = _dot3(K["tri"][...], la)
        return cum, cum[C - 1:C, :] - cum

    o_ssd, o_ret, o_gla = [], [], []
    for c in range(n_chunks):
        sl = slice(c * C, (c + 1) * C)

        cum, rem = cum_and_rest(pr["ssd_la"][sl])
        q, k, v = pr["ssd_q"][sl], pr["ssd_k"][sl], pr["ssd_v"][sl]
        dec = jnp.concatenate(
            [jnp.exp(jnp.where(cmask, col - col.T, -jnp.inf)) for col in _head_cols(cum)], axis=1)
        p = _dot_nt(q.astype(BF16), stack_kv(k, sm256)) * dec
        a_cum = jnp.exp(cum)
        s_old = S_ssd_ref[...]
        o = _dot(p.astype(BF16), stack_kv(v, sm256)) + _dot((q * a_cum).astype(BF16), s_old.astype(BF16))
        ds = _dot_tn((k * jnp.exp(rem)).astype(BF16), v.astype(BF16)) * K["bdm"][...]
        S_ssd_ref[...] = s_old * a_cum[C - 1:C, :] + ds
        o_ssd.append(o)

        q, k, v = pr["rq"][sl], pr["rk"][sl], ret_v[sl]
        p = _dot_nt(q.astype(BF16), stack_kv(k, sm256)) * K["dret"][...]
        s_old = S_ret_ref[...]
        o = (_dot(p.astype(BF16), stack_kv(v, sm256))
             + _dot((q * K["retq"][...]).astype(BF16), s_old.astype(BF16)))
        ds = _dot_tn((k * K["retk"][...]).astype(BF16), v.astype(BF16)) * K["bdm"][...]
        S_ret_ref[...] = s_old * K["rets"][0:1, :] + ds
        o_ret.append(o)

        cum, rem = cum_and_rest(pr["gla_lg"][sl])
        q, k, v = pr["gq"][sl], gla_k[sl], gla_v[sl]
        p = _dot_nt(q.astype(BF16), stack_kv(k, sm128)) * K["gmask"][n_levels]
        for lvl, n in enumerate(level_sizes):
            mid = _mid_cum(cum, n)
            up = K["gup"][lvl] > 0.0
            f = jnp.exp(jnp.where(up, cum - mid, mid - cum))
            ql = jnp.where(up, q * f, 0.0)
            kl = jnp.where(up, 0.0, k * f)
            p = p + _dot_nt(ql.astype(BF16), stack_kv(kl, sm128)) * K["gmask"][lvl]
        a_cum = jnp.exp(cum)
        s_old = S_gla_ref[...]
        o = _dot(p.astype(BF16), stack_kv(v, sm256)) + _dot((q * a_cum).astype(BF16), s_old.astype(BF16))
        ds = _dot_tn((k * jnp.exp(rem)).astype(BF16), v.astype(BF16)) * K["bdg"][...]
        a_col = a_cum.T[:, C - 1:C]
        S_gla_ref[...] = s_old * a_col + ds
        o_gla.append(o)

    cat = lambda parts: parts[0] if n_chunks == 1 else jnp.concatenate(parts, axis=0)
    hmid_ref[...] = _mix_out(x, cat(o_ssd), cat(o_ret), cat(o_gla), y_lru, pr["xs"],
                             _silu(cols(C_Z, 256)), _silu(cols(C_RET + 768, 256)), _silu(cols(C_GR, 256)),
                             W, K["mavg"][...])

    tail_ssd_ref[...] = pr["xbc_raw"][T - 8:, :]
    tail_lru_ref[...] = pr["lru_x"][T - 8:, :]
    hl_ref[...] = l_h[T - 8:, :]


def _feature_major(a, out_ref):
    nb = SAMPLE_B
    for t in range(SAMPLE_T):
        for fb in range(a.shape[1] // 128):
            out_ref[t, fb * 128:(fb + 1) * 128, :] = a[t * nb:(t + 1) * nb, fb * 128:(fb + 1) * 128].T


def _token_major(o_ref):
    f = o_ref.shape[1]
    return jnp.concatenate(
        [jnp.concatenate([o_ref[t, fb * 128:(fb + 1) * 128, :].T for fb in range(f // 128)], axis=1)
         for t in range(SAMPLE_T)], axis=0)


def _sample_pre_kernel(*refs):
    it = iter(refs)
    x_ref, cos_ref, sin_ref, cst_ssd_ref, cst_lru_ref, h0_ref = (next(it) for _ in range(6))
    W = {n: next(it) for n in WEIGHT_NAMES}
    (ssd_q_ref, ssd_k_ref, ssd_v_ref, ssd_a_ref, ret_q_ref, ret_k_ref, ret_v_ref,
     gla_q_ref, gla_k_ref, gla_v_ref, gla_g_ref, side_ref, nconv_ssd_ref, nconv_lru_ref, nh_ref) = (
         next(it) for _ in range(15))
    nb = SAMPLE_B

    u = _dot(_rms(x_ref[...], W["norm_mix"][...]).astype(BF16), W["w_in"][...])
    cols = lambda a, w: u[:, a:a + w]

    def conv(x, w_ref, cst_ref):
        ext = jnp.concatenate([cst_ref[0], cst_ref[1], cst_ref[2], x], axis=0)
        y = w_ref[4:5, :] + x * w_ref[3:4, :]
        for j in range(3):
            y = y + ext[j * nb:(j + SAMPLE_T) * nb, :] * w_ref[j:j + 1, :]
        return y

    pr = _prep(cols, lambda v: conv(v, W["cw_ssd"], cst_ssd_ref), lambda v: conv(v, W["cw_lru"], cst_lru_ref),
               cos_ref[...], sin_ref[...], W)
    nconv_ssd_ref[...] = pr["xbc_raw"][nb:, :].reshape(3, nb, SSD_CONV_DIM)
    nconv_lru_ref[...] = pr["lru_x"][nb:, :].reshape(3, nb, GROUP_WIDTH)

    h = h0_ref[...]
    hs = []
    for t in range(SAMPLE_T):
        h = pr["l_a"][t * nb:(t + 1) * nb, :] * h + pr["l_u"][t * nb:(t + 1) * nb, :]
        hs.append(h)
    nh_ref[...] = h
    y_lru = jnp.concatenate(hs, axis=0) * _gelu_tanh(cols(C_LY, 256))

    _feature_major(pr["ssd_q"], ssd_q_ref)
    _feature_major(pr["ssd_k"], ssd_k_ref)
    _feature_major(pr["ssd_v"], ssd_v_ref)
    _feature_major(jnp.exp(pr["ssd_la"]), ssd_a_ref)
    _feature_major(pr["rq"], ret_q_ref)
    _feature_major(pr["rk"], ret_k_ref)
    _feature_major(cols(C_RET + 512, 256), ret_v_ref)
    _feature_major(pr["gq"], gla_q_ref)
    _feature_major(cols(C_GK, 128), gla_k_ref)
    _feature_major(cols(C_GV, 256), gla_v_ref)
    _feature_major(jnp.exp(pr["gla_lg"]), gla_g_ref)
    side_ref[:, 0:256] = pr["xs"]
    side_ref[:, 256:512] = _silu(cols(C_Z, 256))
    side_ref[:, 512:768] = _silu(cols(C_RET + 768, 256))
    side_ref[:, 768:1024] = _silu(cols(C_GR, 256))
    side_ref[:, 1024:1280] = y_lru


def _sample_state_kernel(ssd_q_ref, ssd_k_ref, ssd_v_ref, ssd_a_ref, ret_q_ref, ret_k_ref, ret_v_ref, gam_ref,
                         gla_q_ref, gla_k_ref, gla_v_ref, gla_g_ref, s_ssd_ref, s_ret_ref, s_gla_ref,
                         *rest):
    o_ssd_ref, o_ret_ref, o_gla_ref, n_ssd_ref, n_ret_ref, n_gla_ref = rest[-6:]

    def run(q_ref, k_ref, v_ref, decay, s_ref, n_ref, o_ref):
        def body(d, acc):
            s = s_ref[d]
            new = []
            for t in range(SAMPLE_T):
                s = decay(t, d) * s + k_ref[t, pl.ds(d, 1), :] * v_ref[t]
                new.append(acc[t] + q_ref[t, pl.ds(d, 1), :] * s)
            n_ref[d] = s
            return tuple(new)

        zero = jnp.zeros((HEAD_DIM, SAMPLE_B), F32)
        acc = lax.fori_loop(0, s_ref.shape[0], body, (zero,) * SAMPLE_T, unroll=2)
        for t in range(SAMPLE_T):
            o_ref[t] = acc[t]

    run(ssd_q_ref, ssd_k_ref, ssd_v_ref, lambda t, d: ssd_a_ref[t, 0:1, :], s_ssd_ref, n_ssd_ref, o_ssd_ref)
    run(ret_q_ref, ret_k_ref, ret_v_ref, lambda t, d: gam_ref[0:1, :], s_ret_ref, n_ret_ref, o_ret_ref)
    run(gla_q_ref, gla_k_ref, gla_v_ref, lambda t, d: gla_g_ref[t, pl.ds(d, 1), :], s_gla_ref, n_gla_ref,
        o_gla_ref)


def _sample_post_kernel(x_ref, side_ref, o_ssd_ref, o_ret_ref, o_gla_ref, vec_ref, wout_ref, mavg_ref,
                        nf_ref, wg_ref, wu_ref, wd_ref, nfin_ref, out_ref, *, final_norm):
    side = lambda i: side_ref[:, i * 256:(i + 1) * 256]
    W = dict(vec=vec_ref, w_out=wout_ref)
    hmid = _mix_out(x_ref[...], _token_major(o_ssd_ref), _token_major(o_ret_ref), _token_major(o_gla_ref),
                    side(4), side(0), side(1), side(2), side(3), W, mavg_ref[...])
    out_ref[...] = _ffn_math(hmid, nf_ref, wg_ref, wu_ref, wd_ref, nfin_ref, final_norm)


def _layer_spec(a, layer, grid_rank):
    idx = (layer,) + (0,) * (a.ndim - 1)
    imap = (lambda i: idx) if grid_rank == 1 else (lambda b, t: idx)
    return pl.BlockSpec((None,) + a.shape[1:], imap, pipeline_mode=pl.Buffered(1))


def _full_spec(a, grid_rank):
    zeros = (0,) * a.ndim
    return pl.BlockSpec(a.shape, (lambda i: zeros) if grid_rank == 1 else (lambda b, t: zeros))


def _ffn(h2d, layer, fw, norm_final, final_norm, tm):
    rows = h2d.shape[0]
    weights = [fw[n] for n in ("norm_ffn", "wg", "wu", "wd")]
    return pl.pallas_call(
        functools.partial(_ffn_kernel, final_norm=final_norm),
        out_shape=jax.ShapeDtypeStruct(h2d.shape, F32),
        grid=(rows // tm,),
        in_specs=[pl.BlockSpec((tm, D_MODEL), lambda i: (i, 0))]
                 + [_layer_spec(a, layer, 1) for a in weights] + [_full_spec(norm_final, 1)],
        out_specs=pl.BlockSpec((tm, D_MODEL), lambda i: (i, 0)),
        compiler_params=pltpu.CompilerParams(
            dimension_semantics=("arbitrary",), vmem_limit_bytes=VMEM_LIMIT_BYTES),
        name="ffn",
    )(h2d, *weights, norm_final)


def _repack_kernel(w_ref, o_ref):
    rows = w_ref.shape[0]
    cp = lambda dst, src, n: o_ref.__setitem__(
        (slice(None), slice(dst, dst + n)), w_ref[:, src:src + n].astype(BF16))
    cp(C_Z, 0, 768)
    dt = [jnp.broadcast_to(w_ref[:, S_DT + h:S_DT + h + 1], (rows, HEAD_DIM)) for h in range(N_HEADS)]
    o_ref[:, C_DT:C_DT + 256] = jnp.concatenate(dt, axis=1).astype(BF16)
    cp(C_RET, S_RET, 1024)
    cp(C_GQ, S_GLA, 512)
    cp(C_GR, S_GR, 256)
    cp(C_LY, S_LRU, 512)
    base = (S_GLR // 128) * 128
    win = pltpu.roll(w_ref[:, base:base + 128], 128 - (S_GLR - base), axis=1)
    lane = lax.broadcasted_iota(jnp.int32, win.shape, 1)
    o_ref[:, C_MISC:C_MISC + 128] = jnp.where(lane < GLA_GATE_RANK, win, 0.0).astype(BF16)


def _repack_w_in(w_in):
    depth = w_in.shape[0]
    rb = 256
    return pl.pallas_call(
        _repack_kernel,
        out_shape=jax.ShapeDtypeStruct((depth, D_MODEL, IN_PACKED), BF16),
        grid=(depth, D_MODEL // rb),
        in_specs=[pl.BlockSpec((None, rb, IN_DIM), lambda l, r: (l, r, 0))],
        out_specs=pl.BlockSpec((None, rb, IN_PACKED), lambda l, r: (l, r, 0)),
        compiler_params=pltpu.CompilerParams(dimension_semantics=("arbitrary", "arbitrary")),
        name="repack_w_in",
    )(w_in)


def _pack_weights(p):
    depth = p["w_in"].shape[0]
    pad_rows = lambda a, n: jnp.concatenate(
        [a, jnp.zeros((depth, n - a.shape[1], a.shape[2]), F32)], axis=1)
    rep = lambda a: jnp.repeat(a, HEAD_DIM, axis=1)
    vec = jnp.stack([rep(p["ssd_dt_bias"]), rep(p["ssd_a_log"]), rep(p["ssd_d"]),
                     p["ssd_norm"], p["ret_norm"], p["gla_norm"],
                     p["lru_b_a"], p["lru_b_x"], p["lru_lambda"]], axis=1)
    eye = jnp.eye(N_HEADS, dtype=F32)
    bd = lambda a: jnp.einsum("lhij,hg->lhigj", a, eye).reshape(depth, GROUP_WIDTH, GROUP_WIDTH).astype(BF16)
    return dict(
        norm_mix=p["norm_mix"][:, None, :], w_in=_repack_w_in(p["w_in"]),
        cw_ssd=pad_rows(jnp.concatenate([p["ssd_conv_w"], p["ssd_conv_b"][:, None, :]], axis=1), 8),
        cw_lru=pad_rows(jnp.concatenate([p["lru_conv_w"], p["lru_conv_b"][:, None, :]], axis=1), 8),
        vec=pad_rows(vec, 16), bg=pad_rows(p["gla_b_gate"][:, None, :], 8),
        w2=pad_rows(p["gla_w_gate2"], 128).astype(BF16),
        wa=bd(p["lru_w_a"]), wx=bd(p["lru_w_x"]), w_out=p["w_out"].astype(BF16))


def _rope_tables(pos):
    half = HEAD_DIM // 2
    inv = ROPE_BASE ** (-jnp.arange(half, dtype=F32) / half)
    ang = pos.astype(F32)[:, None] * inv[None, :]
    cos, sin = jnp.cos(ang), jnp.sin(ang)
    cos = jnp.tile(jnp.concatenate([cos, cos], axis=1), (1, N_HEADS))
    sin = jnp.tile(jnp.concatenate([-sin, sin], axis=1), (1, N_HEADS))
    return cos, sin


def _prompt_mixer(h, cos, sin, layer, wts, consts, tq):
    b, l, _ = h.shape
    weights = [wts[n] for n in WEIGHT_NAMES]
    cvals = [consts[n] for n in CONST_NAMES]
    row = lambda w: pl.BlockSpec((None, tq, w), lambda bi, ti: (bi, ti, 0))
    per_seq = lambda r, w: pl.BlockSpec((None, r, w), lambda bi, ti: (bi, 0, 0))
    out_shapes = (jax.ShapeDtypeStruct((b, l, D_MODEL), F32),
                  jax.ShapeDtypeStruct((b, 8, SSD_CONV_DIM), F32),
                  jax.ShapeDtypeStruct((b, 8, GROUP_WIDTH), F32),
                  jax.ShapeDtypeStruct((b, 8, GROUP_WIDTH), F32),
                  jax.ShapeDtypeStruct((b, 256, 256), F32),
                  jax.ShapeDtypeStruct((b, 256, 256), F32),
                  jax.ShapeDtypeStruct((b, 128, 256), F32))
    return pl.pallas_call(
        functools.partial(_mixer_kernel, n_chunks=tq // CHUNK, level_sizes=_level_sizes(CHUNK)),
        out_shape=out_shapes,
        grid=(b, l // tq),
        in_specs=[row(D_MODEL),
                  pl.BlockSpec((tq, GROUP_WIDTH), lambda bi, ti: (ti, 0)),
                  pl.BlockSpec((tq, GROUP_WIDTH), lambda bi, ti: (ti, 0))]
                 + [_layer_spec(a, layer, 2) for a in weights] + [_full_spec(a, 2) for a in cvals],
        out_specs=(row(D_MODEL), per_seq(8, SSD_CONV_DIM), per_seq(8, GROUP_WIDTH),
                   per_seq(8, GROUP_WIDTH), per_seq(256, 256), per_seq(256, 256), per_seq(128, 256)),
        scratch_shapes=[pltpu.VMEM((8 + tq, SSD_CONV_DIM), F32), pltpu.VMEM((8 + tq, GROUP_WIDTH), F32)],
        compiler_params=pltpu.CompilerParams(
            dimension_semantics=("arbitrary", "arbitrary"), vmem_limit_bytes=VMEM_LIMIT_BYTES),
        name="prompt_mixer",
    )(h, cos, sin, *weights, *cvals)


def _sample_pre(hs, cos, sin, cst_ssd, cst_lru, h0, layer, wts):
    t, b = SAMPLE_T, SAMPLE_B
    weights = [wts[n] for n in WEIGHT_NAMES]
    fm = lambda f: jax.ShapeDtypeStruct((t, f, b), F32)
    out_shapes = (fm(256),) * 7 + (fm(128), fm(128), fm(256), fm(128)) + (
        jax.ShapeDtypeStruct((t * b, 1280), F32),
        jax.ShapeDtypeStruct((3, b, SSD_CONV_DIM), F32), jax.ShapeDtypeStruct((3, b, GROUP_WIDTH), F32),
        jax.ShapeDtypeStruct((b, GROUP_WIDTH), F32))
    lay = lambda a: pl.BlockSpec((None,) + a.shape[1:], lambda i: (layer,) + (0,) * (a.ndim - 1))
    return pl.pallas_call(
        _sample_pre_kernel,
        out_shape=out_shapes,
        grid=(1,),
        in_specs=[_full_spec(hs, 1), _full_spec(cos, 1), _full_spec(sin, 1), lay(cst_ssd), lay(cst_lru), lay(h0)]
                 + [_layer_spec(a, layer, 1) for a in weights],
        out_specs=tuple(pl.BlockSpec(o.shape, lambda i, n=len(o.shape): (0,) * n) for o in out_shapes),
        compiler_params=pltpu.CompilerParams(
            dimension_semantics=("arbitrary",), vmem_limit_bytes=VMEM_LIMIT_BYTES),
        name="sample_pre",
    )(hs, cos, sin, cst_ssd, cst_lru, h0, *weights)


def _sample_state(ops, gam, states, prev_new, layer):
    t, b = SAMPLE_T, SAMPLE_B
    ssd_q, ssd_k, ssd_v, ssd_a, ret_q, ret_k, ret_v, gla_q, gla_k, gla_v, gla_g = ops
    aliased = list(prev_new) if prev_new is not None else []
    blk = lambda f: pl.BlockSpec((t, f // N_HEADS, b), lambda h: (0, h, 0))
    st = lambda dk: pl.BlockSpec((None, None, dk, HEAD_DIM, b), lambda h: (layer, h, 0, 0, 0))
    operands = [ssd_q, ssd_k, ssd_v, ssd_a, ret_q, ret_k, ret_v, gam, gla_q, gla_k, gla_v, gla_g, *states]
    in_specs = ([blk(256)] * 7 + [pl.BlockSpec((None, 8, b), lambda h: (h, 0, 0))]
                + [blk(128), blk(128), blk(256), blk(128), st(64), st(64), st(32)]
                + [pl.BlockSpec(memory_space=pl.ANY) for _ in aliased])
    out_shapes = (jax.ShapeDtypeStruct((t, 256, b), F32),) * 3 + tuple(
        jax.ShapeDtypeStruct(s.shape, F32) for s in states)
    return pl.pallas_call(
        _sample_state_kernel,
        out_shape=out_shapes,
        grid=(N_HEADS,),
        in_specs=in_specs,
        out_specs=(blk(256), blk(256), blk(256), st(64), st(64), st(32)),
        input_output_aliases={len(operands) + k: 3 + k for k in range(len(aliased))},
        compiler_params=pltpu.CompilerParams(
            dimension_semantics=("arbitrary",), vmem_limit_bytes=VMEM_LIMIT_BYTES),
        name="sample_state",
    )(*operands, *aliased)


def _sample_post(hs, side, o_ssd, o_ret, o_gla, layer, wts, mavg, fw, norm_final, final_norm):
    lay = [wts["vec"], wts["w_out"]]
    ffn_w = [fw[n] for n in ("norm_ffn", "wg", "wu", "wd")]
    acts = [hs, side, o_ssd, o_ret, o_gla]
    return pl.pallas_call(
        functools.partial(_sample_post_kernel, final_norm=final_norm),
        out_shape=jax.ShapeDtypeStruct(hs.shape, F32),
        grid=(1,),
        in_specs=[_full_spec(a, 1) for a in acts] + [_layer_spec(a, layer, 1) for a in lay]
                 + [_full_spec(mavg, 1)] + [_layer_spec(a, layer, 1) for a in ffn_w]
                 + [_full_spec(norm_final, 1)],
        out_specs=_full_spec(hs, 1),
        compiler_params=pltpu.CompilerParams(
            dimension_semantics=("arbitrary",), vmem_limit_bytes=VMEM_LIMIT_BYTES),
        name="sample_post",
    )(*acts, *lay, mavg, *ffn_w, norm_final)


def _diag_blocks(s, dk):
    return jnp.stack([s[:, h * dk:(h + 1) * dk, h * HEAD_DIM:(h + 1) * HEAD_DIM]
                      for h in range(N_HEADS)], axis=1)


def kernel(x_prompt, x_sample, state_ssd_conv, state_ssd, state_ret, state_gla, state_lru_conv, state_lru,
           norm_mix, w_in, ssd_conv_w, ssd_conv_b, ssd_dt_bias, ssd_a_log, ssd_d, ssd_norm, ret_norm,
           gla_w_gate2, gla_b_gate, gla_norm, lru_conv_w, lru_conv_b, lru_w_a, lru_b_a, lru_w_x, lru_b_x,
           lru_lambda, w_out, norm_ffn, w_gate, w_up, w_down, norm_final):
    params = dict(norm_mix=norm_mix, w_in=w_in, ssd_conv_w=ssd_conv_w, ssd_conv_b=ssd_conv_b,
                  ssd_dt_bias=ssd_dt_bias, ssd_a_log=ssd_a_log, ssd_d=ssd_d, ssd_norm=ssd_norm,
                  ret_norm=ret_norm, gla_w_gate2=gla_w_gate2, gla_b_gate=gla_b_gate, gla_norm=gla_norm,
                  lru_conv_w=lru_conv_w, lru_conv_b=lru_conv_b, lru_w_a=lru_w_a, lru_b_a=lru_b_a,
                  lru_w_x=lru_w_x, lru_b_x=lru_b_x, lru_lambda=lru_lambda, w_out=w_out)
    depth = w_in.shape[0]
    bp, lp, _ = x_prompt.shape
    bs, ls, _ = x_sample.shape
    assert (bs, ls) == (SAMPLE_B, SAMPLE_T) and lp % CHUNK == 0
    tq = 512 if lp % 512 == 0 else CHUNK
    tm_p = 512 if (bp * lp) % 512 == 0 else CHUNK

    wts = _pack_weights(params)
    fw = dict(norm_ffn=norm_ffn[:, None, :], wg=w_gate.astype(BF16), wu=w_up.astype(BF16),
              wd=w_down.astype(BF16))
    consts = _const_arrays()
    cos_p, sin_p = _rope_tables(jnp.arange(lp, dtype=jnp.int32))
    cos_s, sin_s = _rope_tables(PAST_LEN + jnp.arange(ls, dtype=jnp.int32))
    cos_s, sin_s = jnp.repeat(cos_s, bs, axis=0), jnp.repeat(sin_s, bs, axis=0)
    nfin = norm_final[None]
    gam = jnp.broadcast_to(jnp.exp(_ret_log_gamma())[:, None, None], (N_HEADS, 8, bs))
    cst_ssd = jnp.transpose(state_ssd_conv, (0, 2, 1, 3))
    cst_lru = jnp.transpose(state_lru_conv, (0, 2, 1, 3))
    states_s = tuple(jnp.transpose(a, (0, 2, 3, 4, 1)) for a in (state_ssd, state_ret, state_gla))

    hp = x_prompt
    hs = jnp.transpose(x_sample, (1, 0, 2)).reshape(ls * bs, D_MODEL)
    new_p = [[] for _ in range(6)]
    new_s = [[] for _ in range(3)]
    new_states_s = None
    for i in range(depth):
        last = i == depth - 1

        hmid, tail_ssd, tail_lru, hl, s_ssd, s_ret, s_gla = _prompt_mixer(
            hp, cos_p, sin_p, i, wts, consts, tq)
        hp = _ffn(hmid.reshape(bp * lp, D_MODEL), i, fw, nfin, last, tm_p).reshape(bp, lp, D_MODEL)
        for j, a in enumerate((tail_ssd[:, 5:], _diag_blocks(s_ssd, 64), _diag_blocks(s_ret, 64),
                               _diag_blocks(s_gla, 32), tail_lru[:, 5:], hl[:, 7])):
            new_p[j].append(a)

        *ops, side, nconv_ssd, nconv_lru, nh = _sample_pre(
            hs, cos_s, sin_s, cst_ssd, cst_lru, state_lru, i, wts)
        o_ssd, o_ret, o_gla, *new_states_s = _sample_state(ops, gam, states_s, new_states_s, i)
        hs = _sample_post(hs, side, o_ssd, o_ret, o_gla, i, wts, consts["mavg"], fw, nfin, last)
        for j, a in enumerate((nconv_ssd, nconv_lru, nh)):
            new_s[j].append(a)

    outs_p = [jnp.stack(a, axis=0) for a in new_p]
    s_conv, s_lru_conv, s_lru = [jnp.stack(a, axis=0) for a in new_s]
    n_ssd, n_ret, n_gla = (jnp.transpose(a, (0, 4, 1, 2, 3)) for a in new_states_s)
    y_sample = jnp.transpose(hs.reshape(ls, bs, D_MODEL), (1, 0, 2))
    return (hp, y_sample, *outs_p,
            jnp.transpose(s_conv, (0, 2, 1, 3)), n_ssd, n_ret, n_gla,
            jnp.transpose(s_lru_conv, (0, 2, 1, 3)), s_lru)
```

```python
import functools
import math

import numpy as np
import jax
import jax.numpy as jnp
from jax import lax
from jax.experimental import pallas as pl
from jax.experimental.pallas import tpu as pltpu

F32 = jnp.float32
BF16 = jnp.bfloat16

D_MODEL = 1024
HEAD_DIM = 64
N_HEADS = 4
GROUP_WIDTH = 256
EPS = 1e-6
SSD_CONV_DIM = 512
GLA_KEY_WIDTH = 128
GLA_DK = 32
GLA_GATE_RANK = 16
GLA_GATE_TEMP = 16.0
LRU_C = 8.0
ROPE_BASE = 10000.0
PAST_LEN = 16384
IN_DIM = 3092

C_Z, C_XBC, C_DT, C_RET, C_GQ, C_GK, C_GV, C_GR, C_LY, C_LX, C_MISC, IN_PACKED = (
    0, 256, 768, 1024, 2048, 2176, 2304, 2560, 2816, 3072, 3328, 3456)
S_DT, S_RET, S_GLA, S_GLR, S_GR, S_LRU = 768, 772, 1796, 2308, 2324, 2580

CHUNK = 128
SAMPLE_B = 128
SAMPLE_T = 4
VMEM_LIMIT_BYTES = 56 * 1024 * 1024


def _level_sizes(seg):
    sizes = []
    n = seg
    while n >= 2:
        sizes.append(n)
        n //= 2
    return tuple(sizes)


@functools.lru_cache(maxsize=None)
def _chunk_constants():
    c = CHUNK
    i = np.arange(c)[:, None]
    j = np.arange(c)[None, :]
    up_levels, m_levels = [], []
    for n in _level_sizes(c):
        up_levels.append(np.broadcast_to((i % n) >= n // 2, (c, GLA_KEY_WIDTH)))
        m_levels.append(np.tile((i // n) == (j // n), (1, N_HEADS)))
    m_levels.append(np.tile(i == j, (1, N_HEADS)))
    rows = np.arange(N_HEADS * c)[:, None]
    hh = np.arange(GROUP_WIDTH) // HEAD_DIM
    return dict(
        tri=(j <= i).astype(np.float32),
        gmask=np.stack(m_levels).astype(np.float32),
        gup=np.stack(up_levels).astype(np.float32),
        gsgn=np.where(np.stack(up_levels), 1.0, -1.0).astype(np.float32),
        sm256=((rows // c) == hh[None, :]).astype(np.float32),
        sm128=((rows // c) == (np.arange(GLA_KEY_WIDTH)[None, :] // GLA_DK)).astype(np.float32),
        bdm=(hh[:, None] == hh[None, :]).astype(np.float32),
        bdg=((np.arange(GLA_KEY_WIDTH) // GLA_DK)[:, None] == hh[None, :]).astype(np.float32))


def _ret_log_gamma():
    return jnp.log(1.0 - 2.0 ** (-5.0 - jnp.arange(N_HEADS, dtype=F32)))


def _const_arrays():
    k = _chunk_constants()
    c = CHUNK
    log_gamma = _ret_log_gamma()
    lg256 = jnp.repeat(log_gamma, HEAD_DIM)[None, :]
    pos = jnp.arange(c, dtype=F32)[:, None]
    dpos = pos - pos.T
    tri = jnp.asarray(k["tri"])
    dret = jnp.concatenate(
        [jnp.where(tri > 0, jnp.exp(dpos * log_gamma[h]), 0.0) for h in range(N_HEADS)], axis=1)
    return dict(
        tri=jnp.asarray(k["tri"], BF16),
        cmask=tri, gmask=jnp.asarray(k["gmask"], BF16), gupb=jnp.asarray(k["gup"], BF16),
        gsgn=jnp.asarray(k["gsgn"]),
        retq=jnp.exp((pos + 1.0) * lg256), retk=jnp.exp((c - 1.0 - pos) * lg256),
        rets=jnp.broadcast_to(jnp.exp(float(c) * lg256), (8, GROUP_WIDTH)), dret=dret,
        sm256=jnp.asarray(k["sm256"], BF16), sm128=jnp.asarray(k["sm128"], BF16),
        bdm=jnp.asarray(k["bdm"]), bdg=jnp.asarray(k["bdg"]),
        mavg=jnp.asarray(k["bdm"] / HEAD_DIM, BF16))


CONST_NAMES = ("tri", "cmask", "gmask", "gupb", "gsgn", "retq", "retk", "rets", "dret",
               "sm256", "sm128", "bdm", "bdg", "mavg")
WEIGHT_NAMES = ("norm_mix", "w_in", "cw_ssd", "cw_lru", "vec", "bg", "w2", "wa", "wx", "w_out")

V_DTB, V_ALOG, V_SSDD, V_SSDN, V_RETN, V_GLAN, V_LBA, V_LBX, V_LLAM = range(9)


def _dot(a, b):
    return jnp.dot(a, b, preferred_element_type=F32)


def _dot_nt(a, b):
    return lax.dot_general(a, b, (((1,), (1,)), ((), ())), preferred_element_type=F32)


def _dot_tn(a, b):
    return lax.dot_general(a, b, (((0,), (0,)), ((), ())), preferred_element_type=F32)


def _split3(x):
    hi = x.astype(BF16)
    r1 = x - hi.astype(F32)
    mid = r1.astype(BF16)
    lo = (r1 - mid.astype(F32)).astype(BF16)
    return hi, mid, lo


def _dot3(w01, x):
    hi, mid, lo = _split3(x)
    return _dot(w01, hi) + _dot(w01, mid) + _dot(w01, lo)


def _sigmoid(x):
    return 1.0 / (1.0 + jnp.exp(-x))


def _silu(x):
    return x * _sigmoid(x)


def _log1pexp_neg_abs(x):
    return jnp.log(1.0 + jnp.exp(-jnp.abs(x)))


def _softplus(x):
    return jnp.maximum(x, 0.0) + _log1pexp_neg_abs(x)


def _gelu_tanh(x):
    return 0.5 * x * (1.0 + jnp.tanh(math.sqrt(2.0 / math.pi) * (x + 0.044715 * (x * x * x))))


def _rms(x, w):
    return x * lax.rsqrt(jnp.mean(x * x, axis=-1, keepdims=True) + EPS) * w


def _stack4(x):
    return jnp.concatenate([x, x, x, x], axis=0)


def _causal_conv_carried(x, w_ref, buf_ref):
    rows = x.shape[0]
    buf_ref[8:8 + rows, :] = x
    y = w_ref[4:5, :] + x * w_ref[3:4, :]
    for k in (1, 2, 3):
        y = y + buf_ref[8 - k:8 - k + rows, :] * w_ref[3 - k:4 - k, :]
    buf_ref[0:8, :] = x[rows - 8:, :]
    return y


def _swap_halves(x):
    w = x.shape[1]
    lane = lax.broadcasted_iota(jnp.int32, x.shape, 1)
    return jnp.where((lane & 32) == 0, pltpu.roll(x, w - 32, axis=1), pltpu.roll(x, 32, axis=1))


def _expand_groups(x):
    lane = lax.broadcasted_iota(jnp.int32, x.shape, 1)
    r = pltpu.roll(x, 64, axis=1)
    first = lane < 64
    return jnp.concatenate([jnp.where(first, x, r), jnp.where(first, r, x)], axis=1)


def _head_cols(cum):
    cols = []
    for half in range(2):
        x = cum[:, half * 128:(half + 1) * 128]
        lane = lax.broadcasted_iota(jnp.int32, x.shape, 1)
        r = pltpu.roll(x, 64, axis=1)
        first = lane < 64
        cols.append(jnp.where(first, x, r))
        cols.append(jnp.where(first, r, x))
    return cols


def _mid_cum(cum, n):
    rows, w = cum.shape
    half = n // 2
    if n >= 16:
        blocks = cum.reshape(rows // n, n, w)
        return jnp.broadcast_to(blocks[:, half - 1:half, :], blocks.shape).reshape(rows, w)
    pos = lax.broadcasted_iota(jnp.int32, cum.shape, 0) & (n - 1)
    out = cum
    for p in range(n):
        delta = half - 1 - p
        if delta != 0:
            out = jnp.where(pos == p, pltpu.roll(cum, (-delta) % rows, axis=0), out)
    return out


def _scan_rows(a, u, h_in):
    rows, w = a.shape
    pos = lax.broadcasted_iota(jnp.int32, a.shape, 0) & 7
    for s in (1, 2, 4):
        keep = pos >= s
        a_s = jnp.where(keep, pltpu.roll(a, s, axis=0), 1.0)
        u_s = jnp.where(keep, pltpu.roll(u, s, axis=0), 0.0)
        u = a * u_s + u
        a = a * a_s
    groups = rows // 8
    a3, u3 = a.reshape(groups, 8, w), u.reshape(groups, 8, w)
    carries = []
    h = h_in
    for g in range(groups):
        carries.append(h)
        h = a3[g, 7:8, :] * h + u3[g, 7:8, :]
    carry = jnp.broadcast_to(jnp.stack(carries, axis=0), (groups, 8, w))
    return (u3 + a3 * carry).reshape(rows, w)


def _ffn_math(x, nf_ref, wg_ref, wu_ref, wd_ref, nfin_ref, final_norm):
    hn = _rms(x, nf_ref[...]).astype(BF16)
    g = _dot(hn, wg_ref[...])
    up = _dot(hn, wu_ref[...])
    act = (_silu(g) * up).astype(BF16)
    y = x + _dot(act, wd_ref[...])
    return _rms(y, nfin_ref[...]) if final_norm else y


def _ffn_kernel(x_ref, nf_ref, wg_ref, wu_ref, wd_ref, nfin_ref, o_ref, *, final_norm):
    o_ref[...] = _ffn_math(x_ref[...], nf_ref, wg_ref, wu_ref, wd_ref, nfin_ref, final_norm)


def _prep(cols, conv_ssd_fn, conv_lru_fn, cos, sin, W):
    vec = W["vec"]
    xbc_raw, lru_x = cols(C_XBC, 512), cols(C_LX, 256)
    xbc = _silu(conv_ssd_fn(xbc_raw))
    xs = xbc[:, :256]
    dt = _softplus(cols(C_DT, 256) + vec[V_DTB:V_DTB + 1, :])
    ret_q, ret_k = cols(C_RET, 256), cols(C_RET + 256, 256)
    gate_x = _dot(cols(C_MISC, 128).astype(BF16), W["w2"][...]) + W["bg"][0:1, :]
    xc = conv_lru_fn(lru_x)
    xcb = xc.astype(BF16)
    lr = _sigmoid(_dot(xcb, W["wa"][...]) + vec[V_LBA:V_LBA + 1, :])
    li = _sigmoid(_dot(xcb, W["wx"][...]) + vec[V_LBX:V_LBX + 1, :])
    l_loga = -LRU_C * lr * _softplus(-vec[V_LLAM:V_LLAM + 1, :])
    l_a = jnp.exp(l_loga)
    return dict(
        xbc_raw=xbc_raw, lru_x=lru_x, xs=xs,
        ssd_k=_expand_groups(xbc[:, 256:384]), ssd_q=_expand_groups(xbc[:, 384:512]),
        ssd_la=-dt * jnp.exp(vec[V_ALOG:V_ALOG + 1, :]),
        ssd_v=xs * dt,
        rq=ret_q * cos + _swap_halves(ret_q) * sin,
        rk=(ret_k * cos + _swap_halves(ret_k) * sin) * (HEAD_DIM ** -0.5),
        gla_lg=(jnp.minimum(gate_x, 0.0) - _log1pexp_neg_abs(gate_x)) / GLA_GATE_TEMP,
        gq=cols(C_GQ, 128) * (GLA_DK ** -0.5),
        l_a=l_a,
        l_u=jnp.sqrt(-jnp.tanh(l_loga) * (l_a * l_a + 1.0)) * (li * xc))


def _mix_out(x, o_ssd, o_ret, o_gla, y_lru, xs, g_ssd, g_ret, g_gla, W, mavg):
    vec = W["vec"]
    y_ssd = _rms((o_ssd + xs * vec[V_SSDD:V_SSDD + 1, :]) * g_ssd, vec[V_SSDN:V_SSDN + 1, :])
    mu = _dot(o_ret.astype(BF16), mavg)
    d = o_ret - mu
    var = _dot((d * d).astype(BF16), mavg)
    y_ret = d * lax.rsqrt(var + EPS) * vec[V_RETN:V_RETN + 1, :] * g_ret
    ms = _dot((o_gla * o_gla).astype(BF16), mavg)
    y_gla = o_gla * lax.rsqrt(ms + EPS) * vec[V_GLAN:V_GLAN + 1, :] * g_gla
    ymix = jnp.concatenate([y_ssd, y_ret, y_gla, y_lru], axis=1).astype(BF16)
    return x + _dot(ymix, W["w_out"][...])


def _mixer_kernel(*refs, n_chunks, level_sizes):
    it = iter(refs)
    x_ref, cos_ref, sin_ref = next(it), next(it), next(it)
    W = {n: next(it) for n in WEIGHT_NAMES}
    K = {n: next(it) for n in CONST_NAMES}
    hmid_ref = next(it)
    tail_ssd_ref, tail_lru_ref, hl_ref = next(it), next(it), next(it)
    S_ssd_ref, S_ret_ref, S_gla_ref = next(it), next(it), next(it)
    cbuf_ssd_ref, cbuf_lru_ref = next(it), next(it)

    C = CHUNK
    n_levels = len(level_sizes)

    @pl.when(pl.program_id(1) == 0)
    def _():
        cbuf_ssd_ref[0:8, :] = jnp.zeros((8, SSD_CONV_DIM), F32)
        cbuf_lru_ref[0:8, :] = jnp.zeros((8, GROUP_WIDTH), F32)
        hl_ref[...] = jnp.zeros_like(hl_ref)
        S_ssd_ref[...] = jnp.zeros_like(S_ssd_ref)
        S_ret_ref[...] = jnp.zeros_like(S_ret_ref)
        S_gla_ref[...] = jnp.zeros_like(S_gla_ref)

    x = x_ref[...]
    u = _dot(_rms(x, W["norm_mix"][...]).astype(BF16), W["w_in"][...])
    T = u.shape[0]
    cols = lambda a, w: u[:, a:a + w]

    pr = _prep(cols, lambda v: _causal_conv_carried(v, W["cw_ssd"], cbuf_ssd_ref),
               lambda v: _causal_conv_carried(v, W["cw_lru"], cbuf_lru_ref),
               cos_ref[...], sin_ref[...], W)
    ret_v, gla_k, gla_v = cols(C_RET + 512, 256), cols(C_GK, 128), cols(C_GV, 256)
    l_h = _scan_rows(pr["l_a"], pr["l_u"], hl_ref[7:8, :])
    y_lru = l_h * _gelu_tanh(cols(C_LY, 256))

    cmask = K["cmask"][...] > 0.0
    sm256 = K["sm256"][...]
    sm128 = K["sm128"][...]

    def stack_kv(a, mask):
        return _stack4(a.astype(BF16)) * mask

    def cum_and_rest(la):
        cum = _dot3(K["tri"][...], la)
        return cum, cum[C - 1:C, :] - cum

    o_ssd, o_ret, o_gla = [], [], []
    for c in range(n_chunks):
        sl = slice(c * C, (c + 1) * C)

        cum, rem = cum_and_rest(pr["ssd_la"][sl])
        q, k, v = pr["ssd_q"][sl], pr["ssd_k"][sl], pr["ssd_v"][sl]
        dec = jnp.concatenate(
            [jnp.exp(jnp.where(cmask, col - col.T, -jnp.inf)) for col in _head_cols(cum)], axis=1)
        p = _dot_nt(q.astype(BF16), stack_kv(k, sm256)) * dec
        a_cum = jnp.exp(cum)
        s_old = S_ssd_ref[...]
        o = _dot(p.astype(BF16), stack_kv(v, sm256)) + _dot((q * a_cum).astype(BF16), s_old.astype(BF16))
        ds = _dot_tn((k * jnp.exp(rem)).astype(BF16), v.astype(BF16)) * K["bdm"][...]
        S_ssd_ref[...] = s_old * a_cum[C - 1:C, :] + ds
        o_ssd.append(o)

        q, k, v = pr["rq"][sl], pr["rk"][sl], ret_v[sl]
        p = _dot_nt(q.astype(BF16), stack_kv(k, sm256)) * K["dret"][...]
        s_old = S_ret_ref[...]
        o = (_dot(p.astype(BF16), stack_kv(v, sm256))
             + _dot((q * K["retq"][...]).astype(BF16), s_old.astype(BF16)))
        ds = _dot_tn((k * K["retk"][...]).astype(BF16), v.astype(BF16)) * K["bdm"][...]
        S_ret_ref[...] = s_old * K["rets"][0:1, :] + ds
        o_ret.append(o)

        cum, rem = cum_and_rest(pr["gla_lg"][sl])
        q, k, v = pr["gq"][sl], gla_k[sl], gla_v[sl]
        qb, kb = q.astype(BF16), k.astype(BF16)
        p = _dot_nt(qb, stack_kv(kb, sm128)).astype(BF16) * K["gmask"][n_levels]
        for lvl, n in enumerate(level_sizes):
            mid = _mid_cum(cum, n)
            up = K["gupb"][lvl] > 0.0
            f = jnp.exp((cum - mid) * K["gsgn"][lvl]).astype(BF16)
            zero = jnp.zeros_like(qb)
            ql = jnp.where(up, qb * f, zero)
            kl = jnp.where(up, zero, kb * f)
            sc = _dot_nt(ql, stack_kv(kl, sm128)).astype(BF16)
            p = p + (sc if n == C else sc * K["gmask"][lvl])
        a_cum = jnp.exp(cum)
        s_old = S_gla_ref[...]
        o = _dot(p, stack_kv(v, sm256)) + _dot((q * a_cum).astype(BF16), s_old.astype(BF16))
        ds = _dot_tn((k * jnp.exp(rem)).astype(BF16), v.astype(BF16)) * K["bdg"][...]
        a_col = a_cum.T[:, C - 1:C]
        S_gla_ref[...] = s_old * a_col + ds
        o_gla.append(o)

    cat = lambda parts: parts[0] if n_chunks == 1 else jnp.concatenate(parts, axis=0)
    hmid_ref[...] = _mix_out(x, cat(o_ssd), cat(o_ret), cat(o_gla), y_lru, pr["xs"],
                             _silu(cols(C_Z, 256)), _silu(cols(C_RET + 768, 256)), _silu(cols(C_GR, 256)),
                             W, K["mavg"][...])

    tail_ssd_ref[...] = pr["xbc_raw"][T - 8:, :]
    tail_lru_ref[...] = pr["lru_x"][T - 8:, :]
    hl_ref[...] = l_h[T - 8:, :]


def _feature_major(a, out_ref):
    nb = SAMPLE_B
    for t in range(SAMPLE_T):
        for fb in range(a.shape[1] // 128):
            out_ref[t, fb * 128:(fb + 1) * 128, :] = a[t * nb:(t + 1) * nb, fb * 128:(fb + 1) * 128].T


def _token_major(o_ref):
    f = o_ref.shape[1]
    return jnp.concatenate(
        [jnp.concatenate([o_ref[t, fb * 128:(fb + 1) * 128, :].T for fb in range(f // 128)], axis=1)
         for t in range(SAMPLE_T)], axis=0)


def _sample_pre_kernel(*refs):
    it = iter(refs)
    x_ref, cos_ref, sin_ref, cst_ssd_ref, cst_lru_ref, h0_ref = (next(it) for _ in range(6))
    W = {n: next(it) for n in WEIGHT_NAMES}
    (ssd_q_ref, ssd_k_ref, ssd_v_ref, ssd_a_ref, ret_q_ref, ret_k_ref, ret_v_ref,
     gla_q_ref, gla_k_ref, gla_v_ref, gla_g_ref, side_ref, nconv_ssd_ref, nconv_lru_ref, nh_ref) = (
         next(it) for _ in range(15))
    nb = SAMPLE_B

    u = _dot(_rms(x_ref[...], W["norm_mix"][...]).astype(BF16), W["w_in"][...])
    cols = lambda a, w: u[:, a:a + w]

    def conv(x, w_ref, cst_ref):
        ext = jnp.concatenate([cst_ref[0], cst_ref[1], cst_ref[2], x], axis=0)
        y = w_ref[4:5, :] + x * w_ref[3:4, :]
        for j in range(3):
            y = y + ext[j * nb:(j + SAMPLE_T) * nb, :] * w_ref[j:j + 1, :]
        return y

    pr = _prep(cols, lambda v: conv(v, W["cw_ssd"], cst_ssd_ref), lambda v: conv(v, W["cw_lru"], cst_lru_ref),
               cos_ref[...], sin_ref[...], W)
    nconv_ssd_ref[...] = pr["xbc_raw"][nb:, :].reshape(3, nb, SSD_CONV_DIM)
    nconv_lru_ref[...] = pr["lru_x"][nb:, :].reshape(3, nb, GROUP_WIDTH)

    h = h0_ref[...]
    hs = []
    for t in range(SAMPLE_T):
        h = pr["l_a"][t * nb:(t + 1) * nb, :] * h + pr["l_u"][t * nb:(t + 1) * nb, :]
        hs.append(h)
    nh_ref[...] = h
    y_lru = jnp.concatenate(hs, axis=0) * _gelu_tanh(cols(C_LY, 256))

    _feature_major(pr["ssd_q"], ssd_q_ref)
    _feature_major(pr["ssd_k"], ssd_k_ref)
    _feature_major(pr["ssd_v"], ssd_v_ref)
    _feature_major(jnp.exp(pr["ssd_la"]), ssd_a_ref)
    _feature_major(pr["rq"], ret_q_ref)
    _feature_major(pr["rk"], ret_k_ref)
    _feature_major(cols(C_RET + 512, 256), ret_v_ref)
    _feature_major(pr["gq"], gla_q_ref)
    _feature_major(cols(C_GK, 128), gla_k_ref)
    _feature_major(cols(C_GV, 256), gla_v_ref)
    _feature_major(jnp.exp(pr["gla_lg"]), gla_g_ref)
    side_ref[:, 0:256] = pr["xs"]
    side_ref[:, 256:512] = _silu(cols(C_Z, 256))
    side_ref[:, 512:768] = _silu(cols(C_RET + 768, 256))
    side_ref[:, 768:1024] = _silu(cols(C_GR, 256))
    side_ref[:, 1024:1280] = y_lru


def _sample_state_kernel(ssd_q_ref, ssd_k_ref, ssd_v_ref, ssd_a_ref, ret_q_ref, ret_k_ref, ret_v_ref, gam_ref,
                         gla_q_ref, gla_k_ref, gla_v_ref, gla_g_ref, s_ssd_ref, s_ret_ref, s_gla_ref,
                         *rest):
    o_ssd_ref, o_ret_ref, o_gla_ref, n_ssd_ref, n_ret_ref, n_gla_ref = rest[-6:]

    def run(q_ref, k_ref, v_ref, decay, s_ref, n_ref, o_ref):
        def body(d, acc):
            s = s_ref[d]
            new = []
            for t in range(SAMPLE_T):
                s = decay(t, d) * s + k_ref[t, pl.ds(d, 1), :] * v_ref[t]
                new.append(acc[t] + q_ref[t, pl.ds(d, 1), :] * s)
            n_ref[d] = s
            return tuple(new)

        zero = jnp.zeros((HEAD_DIM, SAMPLE_B), F32)
        acc = lax.fori_loop(0, s_ref.shape[0], body, (zero,) * SAMPLE_T, unroll=2)
        for t in range(SAMPLE_T):
            o_ref[t] = acc[t]

    run(ssd_q_ref, ssd_k_ref, ssd_v_ref, lambda t, d: ssd_a_ref[t, 0:1, :], s_ssd_ref, n_ssd_ref, o_ssd_ref)
    run(ret_q_ref, ret_k_ref, ret_v_ref, lambda t, d: gam_ref[0:1, :], s_ret_ref, n_ret_ref, o_ret_ref)
    run(gla_q_ref, gla_k_ref, gla_v_ref, lambda t, d: gla_g_ref[t, pl.ds(d, 1), :], s_gla_ref, n_gla_ref,
        o_gla_ref)


def _sample_post_kernel(x_ref, side_ref, o_ssd_ref, o_ret_ref, o_gla_ref, vec_ref, wout_ref, mavg_ref,
                        nf_ref, wg_ref, wu_ref, wd_ref, nfin_ref, out_ref, *, final_norm):
    side = lambda i: side_ref[:, i * 256:(i + 1) * 256]
    W = dict(vec=vec_ref, w_out=wout_ref)
    hmid = _mix_out(x_ref[...], _token_major(o_ssd_ref), _token_major(o_ret_ref), _token_major(o_gla_ref),
                    side(4), side(0), side(1), side(2), side(3), W, mavg_ref[...])
    out_ref[...] = _ffn_math(hmid, nf_ref, wg_ref, wu_ref, wd_ref, nfin_ref, final_norm)


def _layer_spec(a, layer, grid_rank):
    idx = (layer,) + (0,) * (a.ndim - 1)
    imap = (lambda i: idx) if grid_rank == 1 else (lambda b, t: idx)
    return pl.BlockSpec((None,) + a.shape[1:], imap, pipeline_mode=pl.Buffered(1))


def _full_spec(a, grid_rank):
    zeros = (0,) * a.ndim
    return pl.BlockSpec(a.shape, (lambda i: zeros) if grid_rank == 1 else (lambda b, t: zeros))


def _ffn(h2d, layer, fw, norm_final, final_norm, tm):
    rows = h2d.shape[0]
    weights = [fw[n] for n in ("norm_ffn", "wg", "wu", "wd")]
    return pl.pallas_call(
        functools.partial(_ffn_kernel, final_norm=final_norm),
        out_shape=jax.ShapeDtypeStruct(h2d.shape, F32),
        grid=(rows // tm,),
        in_specs=[pl.BlockSpec((tm, D_MODEL), lambda i: (i, 0))]
                 + [_layer_spec(a, layer, 1) for a in weights] + [_full_spec(norm_final, 1)],
        out_specs=pl.BlockSpec((tm, D_MODEL), lambda i: (i, 0)),
        compiler_params=pltpu.CompilerParams(
            dimension_semantics=("arbitrary",), vmem_limit_bytes=VMEM_LIMIT_BYTES),
        name="ffn",
    )(h2d, *weights, norm_final)


def _repack_kernel(w_ref, o_ref, *, depth):
    lane = lax.broadcasted_iota(jnp.int32, (D_MODEL, 128), 1)
    for l in range(depth):
        block = lambda src: w_ref[src:src + 128, l, :].T
        put = lambda dst, val: o_ref.__setitem__((l, slice(None), slice(dst, dst + 128)), val.astype(BF16))
        for dst, src, n in ((C_Z, 0, 768), (C_RET, S_RET, 1024), (C_GQ, S_GLA, 512),
                            (C_GR, S_GR, 256), (C_LY, S_LRU, 512)):
            for off in range(0, n, 128):
                put(dst + off, block(src + off))
        dtb = block(S_DT)
        for half in range(2):
            first = jnp.broadcast_to(dtb[:, 2 * half:2 * half + 1], (D_MODEL, 128))
            second = jnp.broadcast_to(dtb[:, 2 * half + 1:2 * half + 2], (D_MODEL, 128))
            put(C_DT + 128 * half, jnp.where(lane < HEAD_DIM, first, second))
        base = (S_GLR // 128) * 128
        win = pltpu.roll(block(base), 128 - (S_GLR - base), axis=1)
        put(C_MISC, jnp.where(lane < GLA_GATE_RANK, win, 0.0))


def _repack_w_in(w_in):
    depth = w_in.shape[0]
    w_cols = jnp.transpose(w_in, (2, 0, 1))
    return pl.pallas_call(
        functools.partial(_repack_kernel, depth=depth),
        out_shape=jax.ShapeDtypeStruct((depth, D_MODEL, IN_PACKED), BF16),
        grid=(1,),
        in_specs=[_full_spec(w_cols, 1)],
        out_specs=pl.BlockSpec((depth, D_MODEL, IN_PACKED), lambda i: (0, 0, 0)),
        compiler_params=pltpu.CompilerParams(
            dimension_semantics=("arbitrary",), vmem_limit_bytes=VMEM_LIMIT_BYTES),
        name="repack_w_in",
    )(w_cols)


def _pack_weights(p):
    depth = p["w_in"].shape[0]
    pad_rows = lambda a, n: jnp.concatenate(
        [a, jnp.zeros((depth, n - a.shape[1], a.shape[2]), F32)], axis=1)
    rep = lambda a: jnp.repeat(a, HEAD_DIM, axis=1)
    vec = jnp.stack([rep(p["ssd_dt_bias"]), rep(p["ssd_a_log"]), rep(p["ssd_d"]),
                     p["ssd_norm"], p["ret_norm"], p["gla_norm"],
                     p["lru_b_a"], p["lru_b_x"], p["lru_lambda"]], axis=1)
    eye = jnp.eye(N_HEADS, dtype=F32)
    bd = lambda a: jnp.einsum("lhij,hg->lhigj", a, eye).reshape(depth, GROUP_WIDTH, GROUP_WIDTH).astype(BF16)
    return dict(
        norm_mix=p["norm_mix"][:, None, :], w_in=_repack_w_in(p["w_in"]),
        cw_ssd=pad_rows(jnp.concatenate([p["ssd_conv_w"], p["ssd_conv_b"][:, None, :]], axis=1), 8),
        cw_lru=pad_rows(jnp.concatenate([p["lru_conv_w"], p["lru_conv_b"][:, None, :]], axis=1), 8),
        vec=pad_rows(vec, 16), bg=pad_rows(p["gla_b_gate"][:, None, :], 8),
        w2=pad_rows(p["gla_w_gate2"], 128).astype(BF16),
        wa=bd(p["lru_w_a"]), wx=bd(p["lru_w_x"]), w_out=p["w_out"].astype(BF16))


def _rope_tables(pos):
    half = HEAD_DIM // 2
    inv = ROPE_BASE ** (-jnp.arange(half, dtype=F32) / half)
    ang = pos.astype(F32)[:, None] * inv[None, :]
    cos, sin = jnp.cos(ang), jnp.sin(ang)
    cos = jnp.tile(jnp.concatenate([cos, cos], axis=1), (1, N_HEADS))
    sin = jnp.tile(jnp.concatenate([-sin, sin], axis=1), (1, N_HEADS))
    return cos, sin


def _prompt_mixer(h, cos, sin, layer, wts, consts, tq):
    b, l, _ = h.shape
    weights = [wts[n] for n in WEIGHT_NAMES]
    cvals = [consts[n] for n in CONST_NAMES]
    row = lambda w: pl.BlockSpec((None, tq, w), lambda bi, ti: (bi, ti, 0))
    per_seq = lambda r, w: pl.BlockSpec((None, r, w), lambda bi, ti: (bi, 0, 0))
    out_shapes = (jax.ShapeDtypeStruct((b, l, D_MODEL), F32),
                  jax.ShapeDtypeStruct((b, 8, SSD_CONV_DIM), F32),
                  jax.ShapeDtypeStruct((b, 8, GROUP_WIDTH), F32),
                  jax.ShapeDtypeStruct((b, 8, GROUP_WIDTH), F32),
                  jax.ShapeDtypeStruct((b, 256, 256), F32),
                  jax.ShapeDtypeStruct((b, 256, 256), F32),
                  jax.ShapeDtypeStruct((b, 128, 256), F32))
    return pl.pallas_call(
        functools.partial(_mixer_kernel, n_chunks=tq // CHUNK, level_sizes=_level_sizes(CHUNK)),
        out_shape=out_shapes,
        grid=(b, l // tq),
        in_specs=[row(D_MODEL),
                  pl.BlockSpec((tq, GROUP_WIDTH), lambda bi, ti: (ti, 0)),
                  pl.BlockSpec((tq, GROUP_WIDTH), lambda bi, ti: (ti, 0))]
                 + [_layer_spec(a, layer, 2) for a in weights] + [_full_spec(a, 2) for a in cvals],
        out_specs=(row(D_MODEL), per_seq(8, SSD_CONV_DIM), per_seq(8, GROUP_WIDTH),
                   per_seq(8, GROUP_WIDTH), per_seq(256, 256), per_seq(256, 256), per_seq(128, 256)),
        scratch_shapes=[pltpu.VMEM((8 + tq, SSD_CONV_DIM), F32), pltpu.VMEM((8 + tq, GROUP_WIDTH), F32)],
        compiler_params=pltpu.CompilerParams(
            dimension_semantics=("arbitrary", "arbitrary"), vmem_limit_bytes=VMEM_LIMIT_BYTES),
        name="prompt_mixer",
    )(h, cos, sin, *weights, *cvals)


def _sample_pre(hs, cos, sin, cst_ssd, cst_lru, h0, layer, wts):
    t, b = SAMPLE_T, SAMPLE_B
    weights = [wts[n] for n in WEIGHT_NAMES]
    fm = lambda f: jax.ShapeDtypeStruct((t, f, b), F32)
    out_shapes = (fm(256),) * 7 + (fm(128), fm(128), fm(256), fm(128)) + (
        jax.ShapeDtypeStruct((t * b, 1280), F32),
        jax.ShapeDtypeStruct((3, b, SSD_CONV_DIM), F32), jax.ShapeDtypeStruct((3, b, GROUP_WIDTH), F32),
        jax.ShapeDtypeStruct((b, GROUP_WIDTH), F32))
    lay = lambda a: pl.BlockSpec((None,) + a.shape[1:], lambda i: (layer,) + (0,) * (a.ndim - 1))
    return pl.pallas_call(
        _sample_pre_kernel,
        out_shape=out_shapes,
        grid=(1,),
        in_specs=[_full_spec(hs, 1), _full_spec(cos, 1), _full_spec(sin, 1), lay(cst_ssd), lay(cst_lru), lay(h0)]
                 + [_layer_spec(a, layer, 1) for a in weights],
        out_specs=tuple(pl.BlockSpec(o.shape, lambda i, n=len(o.shape): (0,) * n) for o in out_shapes),
        compiler_params=pltpu.CompilerParams(
            dimension_semantics=("arbitrary",), vmem_limit_bytes=VMEM_LIMIT_BYTES),
        name="sample_pre",
    )(hs, cos, sin, cst_ssd, cst_lru, h0, *weights)


def _sample_state(ops, gam, states, prev_new, layer):
    t, b = SAMPLE_T, SAMPLE_B
    ssd_q, ssd_k, ssd_v, ssd_a, ret_q, ret_k, ret_v, gla_q, gla_k, gla_v, gla_g = ops
    aliased = list(prev_new) if prev_new is not None else []
    blk = lambda f: pl.BlockSpec((t, f // N_HEADS, b), lambda h: (0, h, 0))
    st = lambda dk: pl.BlockSpec((None, None, dk, HEAD_DIM, b), lambda h: (layer, h, 0, 0, 0))
    operands = [ssd_q, ssd_k, ssd_v, ssd_a, ret_q, ret_k, ret_v, gam, gla_q, gla_k, gla_v, gla_g, *states]
    in_specs = ([blk(256)] * 7 + [pl.BlockSpec((None, 8, b), lambda h: (h, 0, 0))]
                + [blk(128), blk(128), blk(256), blk(128), st(64), st(64), st(32)]
                + [pl.BlockSpec(memory_space=pl.ANY) for _ in aliased])
    out_shapes = (jax.ShapeDtypeStruct((t, 256, b), F32),) * 3 + tuple(
        jax.ShapeDtypeStruct(s.shape, F32) for s in states)
    return pl.pallas_call(
        _sample_state_kernel,
        out_shape=out_shapes,
        grid=(N_HEADS,),
        in_specs=in_specs,
        out_specs=(blk(256), blk(256), blk(256), st(64), st(64), st(32)),
        input_output_aliases={len(operands) + k: 3 + k for k in range(len(aliased))},
        compiler_params=pltpu.CompilerParams(
            dimension_semantics=("arbitrary",), vmem_limit_bytes=VMEM_LIMIT_BYTES),
        name="sample_state",
    )(*operands, *aliased)


def _sample_post(hs, side, o_ssd, o_ret, o_gla, layer, wts, mavg, fw, norm_final, final_norm):
    lay = [wts["vec"], wts["w_out"]]
    ffn_w = [fw[n] for n in ("norm_ffn", "wg", "wu", "wd")]
    acts = [hs, side, o_ssd, o_ret, o_gla]
    return pl.pallas_call(
        functools.partial(_sample_post_kernel, final_norm=final_norm),
        out_shape=jax.ShapeDtypeStruct(hs.shape, F32),
        grid=(1,),
        in_specs=[_full_spec(a, 1) for a in acts] + [_layer_spec(a, layer, 1) for a in lay]
                 + [_full_spec(mavg, 1)] + [_layer_spec(a, layer, 1) for a in ffn_w]
                 + [_full_spec(norm_final, 1)],
        out_specs=_full_spec(hs, 1),
        compiler_params=pltpu.CompilerParams(
            dimension_semantics=("arbitrary",), vmem_limit_bytes=VMEM_LIMIT_BYTES),
        name="sample_post",
    )(*acts, *lay, mavg, *ffn_w, norm_final)


def _diag_blocks(s, dk):
    return jnp.stack([s[:, h * dk:(h + 1) * dk, h * HEAD_DIM:(h + 1) * HEAD_DIM]
                      for h in range(N_HEADS)], axis=1)


def kernel(x_prompt, x_sample, state_ssd_conv, state_ssd, state_ret, state_gla, state_lru_conv, state_lru,
           norm_mix, w_in, ssd_conv_w, ssd_conv_b, ssd_dt_bias, ssd_a_log, ssd_d, ssd_norm, ret_norm,
           gla_w_gate2, gla_b_gate, gla_norm, lru_conv_w, lru_conv_b, lru_w_a, lru_b_a, lru_w_x, lru_b_x,
           lru_lambda, w_out, norm_ffn, w_gate, w_up, w_down, norm_final):
    params = dict(norm_mix=norm_mix, w_in=w_in, ssd_conv_w=ssd_conv_w, ssd_conv_b=ssd_conv_b,
                  ssd_dt_bias=ssd_dt_bias, ssd_a_log=ssd_a_log, ssd_d=ssd_d, ssd_norm=ssd_norm,
                  ret_norm=ret_norm, gla_w_gate2=gla_w_gate2, gla_b_gate=gla_b_gate, gla_norm=gla_norm,
                  lru_conv_w=lru_conv_w, lru_conv_b=lru_conv_b, lru_w_a=lru_w_a, lru_b_a=lru_b_a,
                  lru_w_x=lru_w_x, lru_b_x=lru_b_x, lru_lambda=lru_lambda, w_out=w_out)
    depth = w_in.shape[0]
    bp, lp, _ = x_prompt.shape
    bs, ls, _ = x_sample.shape
    assert (bs, ls) == (SAMPLE_B, SAMPLE_T) and lp % CHUNK == 0
    tq = 512 if lp % 512 == 0 else CHUNK
    tm_p = 512 if (bp * lp) % 512 == 0 else CHUNK

    wts = _pack_weights(params)
    fw = dict(norm_ffn=norm_ffn[:, None, :], wg=w_gate.astype(BF16), wu=w_up.astype(BF16),
              wd=w_down.astype(BF16))
    consts = _const_arrays()
    cos_p, sin_p = _rope_tables(jnp.arange(lp, dtype=jnp.int32))
    cos_s, sin_s = _rope_tables(PAST_LEN + jnp.arange(ls, dtype=jnp.int32))
    cos_s, sin_s = jnp.repeat(cos_s, bs, axis=0), jnp.repeat(sin_s, bs, axis=0)
    nfin = norm_final[None]
    gam = jnp.broadcast_to(jnp.exp(_ret_log_gamma())[:, None, None], (N_HEADS, 8, bs))
    cst_ssd = jnp.transpose(state_ssd_conv, (0, 2, 1, 3))
    cst_lru = jnp.transpose(state_lru_conv, (0, 2, 1, 3))
    states_s = tuple(jnp.transpose(a, (0, 2, 3, 4, 1)) for a in (state_ssd, state_ret, state_gla))

    hp = x_prompt
    hs = jnp.transpose(x_sample, (1, 0, 2)).reshape(ls * bs, D_MODEL)
    new_p = [[] for _ in range(6)]
    new_s = [[] for _ in range(3)]
    new_states_s = None
    for i in range(depth):
        last = i == depth - 1

        hmid, tail_ssd, tail_lru, hl, s_ssd, s_ret, s_gla = _prompt_mixer(
            hp, cos_p, sin_p, i, wts, consts, tq)
        hp = _ffn(hmid.reshape(bp * lp, D_MODEL), i, fw, nfin, last, tm_p).reshape(bp, lp, D_MODEL)
        for j, a in enumerate((tail_ssd[:, 5:], _diag_blocks(s_ssd, 64), _diag_blocks(s_ret, 64),
                               _diag_blocks(s_gla, 32), tail_lru[:, 5:], hl[:, 7])):
            new_p[j].append(a)

        *ops, side, nconv_ssd, nconv_lru, nh = _sample_pre(
            hs, cos_s, sin_s, cst_ssd, cst_lru, state_lru, i, wts)
        o_ssd, o_ret, o_gla, *new_states_s = _sample_state(ops, gam, states_s, new_states_s, i)
        hs = _sample_post(hs, side, o_ssd, o_ret, o_gla, i, wts, consts["mavg"], fw, nfin, last)
        for j, a in enumerate((nconv_ssd, nconv_lru, nh)):
            new_s[j].append(a)

    outs_p = [jnp.stack(a, axis=0) for a in new_p]
    s_conv, s_lru_conv, s_lru = [jnp.stack(a, axis=0) for a in new_s]
    n_ssd, n_ret, n_gla = (jnp.transpose(a, (0, 4, 1, 2, 3)) for a in new_states_s)
    y_sample = jnp.transpose(hs.reshape(ls, bs, D_MODEL), (1, 0, 2))
    return (hp, y_sample, *outs_p,
            jnp.transpose(s_conv, (0, 2, 1, 3)), n_ssd, n_ret, n_gla,
            jnp.transpose(s_lru_conv, (0, 2, 1, 3)), s_lru)
```

```python
import functools
import math

import numpy as np
import jax
import jax.numpy as jnp
from jax import lax
from jax.experimental import pallas as pl
from jax.experimental.pallas import tpu as pltpu

F32 = jnp.float32
BF16 = jnp.bfloat16

D_MODEL = 1024
HEAD_DIM = 64
N_HEADS = 4
GROUP_WIDTH = 256
EPS = 1e-6
SSD_CONV_DIM = 512
GLA_KEY_WIDTH = 128
GLA_DK = 32
GLA_GATE_RANK = 16
GLA_GATE_TEMP = 16.0
LRU_C = 8.0
ROPE_BASE = 10000.0
PAST_LEN = 16384
IN_DIM = 3092

C_Z, C_XBC, C_DT, C_RET, C_GQ, C_GK, C_GV, C_GR, C_LY, C_LX, C_MISC, IN_PACKED = (
    0, 256, 768, 1024, 2048, 2176, 2304, 2560, 2816, 3072, 3328, 3456)
S_DT, S_RET, S_GLA, S_GLR, S_GR, S_LRU = 768, 772, 1796, 2308, 2324, 2580

CHUNK = 128
PROJ_ROWS, PROJ_COLS = 128, 256
SAMPLE_B = 128
SAMPLE_T = 4
VMEM_LIMIT_BYTES = 56 * 1024 * 1024


def _level_sizes(seg):
    sizes = []
    n = seg
    while n >= 2:
        sizes.append(n)
        n //= 2
    return tuple(sizes)


@functools.lru_cache(maxsize=None)
def _chunk_constants():
    c = CHUNK
    i = np.arange(c)[:, None]
    j = np.arange(c)[None, :]
    up_levels, m_levels = [], []
    for n in _level_sizes(c):
        up_levels.append(np.broadcast_to((i % n) >= n // 2, (c, GLA_KEY_WIDTH)))
        m_levels.append(np.tile((i // n) == (j // n), (1, N_HEADS)))
    m_levels.append(np.tile(i == j, (1, N_HEADS)))
    rows = np.arange(N_HEADS * c)[:, None]
    hh = np.arange(GROUP_WIDTH) // HEAD_DIM
    return dict(
        tri=(j <= i).astype(np.float32),
        gmask=np.stack(m_levels).astype(np.float32),
        gup=np.stack(up_levels).astype(np.float32),
        gsgn=np.where(np.stack(up_levels), 1.0, -1.0).astype(np.float32),
        sm256=((rows // c) == hh[None, :]).astype(np.float32),
        sm128=((rows // c) == (np.arange(GLA_KEY_WIDTH)[None, :] // GLA_DK)).astype(np.float32),
        bdm=(hh[:, None] == hh[None, :]).astype(np.float32),
        bdg=((np.arange(GLA_KEY_WIDTH) // GLA_DK)[:, None] == hh[None, :]).astype(np.float32))


def _ret_log_gamma():
    return jnp.log(1.0 - 2.0 ** (-5.0 - jnp.arange(N_HEADS, dtype=F32)))


def _const_arrays():
    k = _chunk_constants()
    c = CHUNK
    log_gamma = _ret_log_gamma()
    lg256 = jnp.repeat(log_gamma, HEAD_DIM)[None, :]
    pos = jnp.arange(c, dtype=F32)[:, None]
    dpos = pos - pos.T
    tri = jnp.asarray(k["tri"])
    dret = jnp.concatenate(
        [jnp.where(tri > 0, jnp.exp(dpos * log_gamma[h]), 0.0) for h in range(N_HEADS)], axis=1)
    return dict(
        tri=jnp.asarray(k["tri"], BF16),
        cmask=tri, gmask=jnp.asarray(k["gmask"], BF16), gupb=jnp.asarray(k["gup"], BF16),
        gsgn=jnp.asarray(k["gsgn"]),
        retq=jnp.exp((pos + 1.0) * lg256), retk=jnp.exp((c - 1.0 - pos) * lg256),
        rets=jnp.broadcast_to(jnp.exp(float(c) * lg256), (8, GROUP_WIDTH)), dret=dret,
        sm256=jnp.asarray(k["sm256"], BF16), sm128=jnp.asarray(k["sm128"], BF16),
        bdm=jnp.asarray(k["bdm"]), bdg=jnp.asarray(k["bdg"]),
        mavg=jnp.asarray(k["bdm"] / HEAD_DIM, BF16))


CONST_NAMES = ("tri", "cmask", "gmask", "gupb", "gsgn", "retq", "retk", "rets", "dret",
               "sm256", "sm128", "bdm", "bdg", "mavg")
WEIGHT_NAMES = ("norm_mix", "w_in", "cw_ssd", "cw_lru", "vec", "bg", "w2", "wa", "wx", "w_out")

V_DTB, V_ALOG, V_SSDD, V_SSDN, V_RETN, V_GLAN, V_LBA, V_LBX, V_LLAM = range(9)


def _dot(a, b):
    return jnp.dot(a, b, preferred_element_type=F32)


def _dot_nt(a, b):
    return lax.dot_general(a, b, (((1,), (1,)), ((), ())), preferred_element_type=F32)


def _dot_tn(a, b):
    return lax.dot_general(a, b, (((0,), (0,)), ((), ())), preferred_element_type=F32)


def _split3(x):
    hi = x.astype(BF16)
    r1 = x - hi.astype(F32)
    mid = r1.astype(BF16)
    lo = (r1 - mid.astype(F32)).astype(BF16)
    return hi, mid, lo


def _dot3(w01, x):
    hi, mid, lo = _split3(x)
    return _dot(w01, hi) + _dot(w01, mid) + _dot(w01, lo)


def _sigmoid(x):
    return 1.0 / (1.0 + jnp.exp(-x))


def _silu(x):
    return x * _sigmoid(x)


def _log1pexp_neg_abs(x):
    return jnp.log(1.0 + jnp.exp(-jnp.abs(x)))


def _softplus(x):
    return jnp.maximum(x, 0.0) + _log1pexp_neg_abs(x)


def _gelu_tanh(x):
    return 0.5 * x * (1.0 + jnp.tanh(math.sqrt(2.0 / math.pi) * (x + 0.044715 * (x * x * x))))


def _rms(x, w):
    return x * lax.rsqrt(jnp.mean(x * x, axis=-1, keepdims=True) + EPS) * w


def _stack4(x):
    return jnp.concatenate([x, x, x, x], axis=0)


def _causal_conv_carried(x, w_ref, buf_ref):
    rows = x.shape[0]
    buf_ref[8:8 + rows, :] = x
    y = w_ref[4:5, :] + x * w_ref[3:4, :]
    for k in (1, 2, 3):
        y = y + buf_ref[8 - k:8 - k + rows, :] * w_ref[3 - k:4 - k, :]
    buf_ref[0:8, :] = x[rows - 8:, :]
    return y


def _swap_halves(x):
    w = x.shape[1]
    lane = lax.broadcasted_iota(jnp.int32, x.shape, 1)
    return jnp.where((lane & 32) == 0, pltpu.roll(x, w - 32, axis=1), pltpu.roll(x, 32, axis=1))


def _expand_groups(x):
    lane = lax.broadcasted_iota(jnp.int32, x.shape, 1)
    r = pltpu.roll(x, 64, axis=1)
    first = lane < 64
    return jnp.concatenate([jnp.where(first, x, r), jnp.where(first, r, x)], axis=1)


def _head_cols(cum):
    cols = []
    for half in range(2):
        x = cum[:, half * 128:(half + 1) * 128]
        lane = lax.broadcasted_iota(jnp.int32, x.shape, 1)
        r = pltpu.roll(x, 64, axis=1)
        first = lane < 64
        cols.append(jnp.where(first, x, r))
        cols.append(jnp.where(first, r, x))
    return cols


def _mid_cum(cum, n):
    rows, w = cum.shape
    half = n // 2
    if n >= 16:
        blocks = cum.reshape(rows // n, n, w)
        return jnp.broadcast_to(blocks[:, half - 1:half, :], blocks.shape).reshape(rows, w)
    pos = lax.broadcasted_iota(jnp.int32, cum.shape, 0) & (n - 1)
    out = cum
    for p in range(n):
        delta = half - 1 - p
        if delta != 0:
            out = jnp.where(pos == p, pltpu.roll(cum, (-delta) % rows, axis=0), out)
    return out


def _scan_rows(a, u, h_in):
    rows, w = a.shape
    pos = lax.broadcasted_iota(jnp.int32, a.shape, 0) & 7
    for s in (1, 2, 4):
        keep = pos >= s
        a_s = jnp.where(keep, pltpu.roll(a, s, axis=0), 1.0)
        u_s = jnp.where(keep, pltpu.roll(u, s, axis=0), 0.0)
        u = a * u_s + u
        a = a * a_s
    groups = rows // 8
    a3, u3 = a.reshape(groups, 8, w), u.reshape(groups, 8, w)
    carries = []
    h = h_in
    for g in range(groups):
        carries.append(h)
        h = a3[g, 7:8, :] * h + u3[g, 7:8, :]
    carry = jnp.broadcast_to(jnp.stack(carries, axis=0), (groups, 8, w))
    return (u3 + a3 * carry).reshape(rows, w)


def _ffn_math(x, nf_ref, wg_ref, wu_ref, wd_ref, nfin_ref, final_norm):
    hn = _rms(x, nf_ref[...]).astype(BF16)
    g = _dot(hn, wg_ref[...])
    up = _dot(hn, wu_ref[...])
    act = (_silu(g) * up).astype(BF16)
    y = x + _dot(act, wd_ref[...])
    return _rms(y, nfin_ref[...]) if final_norm else y


def _ffn_kernel(x_ref, nf_ref, wg_ref, wu_ref, wd_ref, nfin_ref, o_ref, *, final_norm):
    o_ref[...] = _ffn_math(x_ref[...], nf_ref, wg_ref, wu_ref, wd_ref, nfin_ref, final_norm)


def _prep(cols, conv_ssd_fn, conv_lru_fn, cos, sin, W):
    vec = W["vec"]
    xbc_raw, lru_x = cols(C_XBC, 512), cols(C_LX, 256)
    xbc = _silu(conv_ssd_fn(xbc_raw))
    xs = xbc[:, :256]
    dt = _softplus(cols(C_DT, 256) + vec[V_DTB:V_DTB + 1, :])
    ret_q, ret_k = cols(C_RET, 256), cols(C_RET + 256, 256)
    gate_x = _dot(cols(C_MISC, 128).astype(BF16), W["w2"][...]) + W["bg"][0:1, :]
    xc = conv_lru_fn(lru_x)
    xcb = xc.astype(BF16)
    lr = _sigmoid(_dot(xcb, W["wa"][...]) + vec[V_LBA:V_LBA + 1, :])
    li = _sigmoid(_dot(xcb, W["wx"][...]) + vec[V_LBX:V_LBX + 1, :])
    l_loga = -LRU_C * lr * _softplus(-vec[V_LLAM:V_LLAM + 1, :])
    l_a = jnp.exp(l_loga)
    return dict(
        xbc_raw=xbc_raw, lru_x=lru_x, xs=xs,
        ssd_k=_expand_groups(xbc[:, 256:384]), ssd_q=_expand_groups(xbc[:, 384:512]),
        ssd_la=-dt * jnp.exp(vec[V_ALOG:V_ALOG + 1, :]),
        ssd_v=xs * dt,
        rq=ret_q * cos + _swap_halves(ret_q) * sin,
        rk=(ret_k * cos + _swap_halves(ret_k) * sin) * (HEAD_DIM ** -0.5),
        gla_lg=(jnp.minimum(gate_x, 0.0) - _log1pexp_neg_abs(gate_x)) / GLA_GATE_TEMP,
        gq=cols(C_GQ, 128) * (GLA_DK ** -0.5),
        l_a=l_a,
        l_u=jnp.sqrt(-jnp.tanh(l_loga) * (l_a * l_a + 1.0)) * (li * xc))


def _mix_out(x, o_ssd, o_ret, o_gla, y_lru, xs, g_ssd, g_ret, g_gla, W, mavg):
    vec = W["vec"]
    y_ssd = _rms((o_ssd + xs * vec[V_SSDD:V_SSDD + 1, :]) * g_ssd, vec[V_SSDN:V_SSDN + 1, :])
    mu = _dot(o_ret.astype(BF16), mavg)
    d = o_ret - mu
    var = _dot((d * d).astype(BF16), mavg)
    y_ret = d * lax.rsqrt(var + EPS) * vec[V_RETN:V_RETN + 1, :] * g_ret
    ms = _dot((o_gla * o_gla).astype(BF16), mavg)
    y_gla = o_gla * lax.rsqrt(ms + EPS) * vec[V_GLAN:V_GLAN + 1, :] * g_gla
    ymix = jnp.concatenate([y_ssd, y_ret, y_gla, y_lru], axis=1).astype(BF16)
    return x + _dot(ymix, W["w_out"][...])


def _mixer_kernel(*refs, n_chunks, level_sizes):
    it = iter(refs)
    x_ref, cos_ref, sin_ref = next(it), next(it), next(it)
    W = {n: next(it) for n in WEIGHT_NAMES}
    K = {n: next(it) for n in CONST_NAMES}
    hmid_ref = next(it)
    tail_ssd_ref, tail_lru_ref, hl_ref = next(it), next(it), next(it)
    S_ssd_ref, S_ret_ref, S_gla_ref = next(it), next(it), next(it)
    cbuf_ssd_ref, cbuf_lru_ref = next(it), next(it)

    C = CHUNK
    n_levels = len(level_sizes)

    @pl.when(pl.program_id(1) == 0)
    def _():
        cbuf_ssd_ref[0:8, :] = jnp.zeros((8, SSD_CONV_DIM), F32)
        cbuf_lru_ref[0:8, :] = jnp.zeros((8, GROUP_WIDTH), F32)
        hl_ref[...] = jnp.zeros_like(hl_ref)
        S_ssd_ref[...] = jnp.zeros_like(S_ssd_ref)
        S_ret_ref[...] = jnp.zeros_like(S_ret_ref)
        S_gla_ref[...] = jnp.zeros_like(S_gla_ref)

    x = x_ref[...]
    T = x.shape[0]
    xn = _rms(x, W["norm_mix"][...]).astype(BF16)
    u_blocks = {}

    def cols(a, w):
        parts = []
        for c0 in range(a, a + w, 128):
            if c0 not in u_blocks:
                base = (c0 // PROJ_COLS) * PROJ_COLS
                width = min(PROJ_COLS, IN_PACKED - base)
                wc = W["w_in"][:, base:base + width]
                full = jnp.concatenate(
                    [_dot(xn[r:r + PROJ_ROWS, :], wc) for r in range(0, T, PROJ_ROWS)], axis=0)
                for k in range(width // 128):
                    u_blocks[base + 128 * k] = full[:, 128 * k:128 * (k + 1)]
            parts.append(u_blocks[c0])
        return parts[0] if len(parts) == 1 else jnp.concatenate(parts, axis=1)

    pr = _prep(cols, lambda v: _causal_conv_carried(v, W["cw_ssd"], cbuf_ssd_ref),
               lambda v: _causal_conv_carried(v, W["cw_lru"], cbuf_lru_ref),
               cos_ref[...], sin_ref[...], W)
    ret_v, gla_k, gla_v = cols(C_RET + 512, 256), cols(C_GK, 128), cols(C_GV, 256)
    l_h = _scan_rows(pr["l_a"], pr["l_u"], hl_ref[7:8, :])
    y_lru = l_h * _gelu_tanh(cols(C_LY, 256))

    cmask = K["cmask"][...] > 0.0
    sm256 = K["sm256"][...]
    sm128 = K["sm128"][...]

    def stack_kv(a, mask):
        return _stack4(a.astype(BF16)) * mask

    def cum_and_rest(la):
        cum = _dot3(K["tri"][...], la)
        return cum, cum[C - 1:C, :] - cum

    o_ssd, o_ret, o_gla = [], [], []
    for c in range(n_chunks):
        sl = slice(c * C, (c + 1) * C)

        cum, rem = cum_and_rest(pr["ssd_la"][sl])
        q, k, v = pr["ssd_q"][sl], pr["ssd_k"][sl], pr["ssd_v"][sl]
        dec = jnp.concatenate(
            [jnp.exp(jnp.where(cmask, col - col.T, -jnp.inf)) for col in _head_cols(cum)], axis=1)
        p = _dot_nt(q.astype(BF16), stack_kv(k, sm256)) * dec
        a_cum = jnp.exp(cum)
        s_old = S_ssd_ref[...]
        o = _dot(p.astype(BF16), stack_kv(v, sm256)) + _dot((q * a_cum).astype(BF16), s_old.astype(BF16))
        ds = _dot_tn((k * jnp.exp(rem)).astype(BF16), v.astype(BF16)) * K["bdm"][...]
        S_ssd_ref[...] = s_old * a_cum[C - 1:C, :] + ds
        o_ssd.append(o)

        q, k, v = pr["rq"][sl], pr["rk"][sl], ret_v[sl]
        p = _dot_nt(q.astype(BF16), stack_kv(k, sm256)) * K["dret"][...]
        s_old = S_ret_ref[...]
        o = (_dot(p.astype(BF16), stack_kv(v, sm256))
             + _dot((q * K["retq"][...]).astype(BF16), s_old.astype(BF16)))
        ds = _dot_tn((k * K["retk"][...]).astype(BF16), v.astype(BF16)) * K["bdm"][...]
        S_ret_ref[...] = s_old * K["rets"][0:1, :] + ds
        o_ret.append(o)

        cum, rem = cum_and_rest(pr["gla_lg"][sl])
        q, k, v = pr["gq"][sl], gla_k[sl], gla_v[sl]
        qb, kb = q.astype(BF16), k.astype(BF16)
        p = _dot_nt(qb, stack_kv(kb, sm128)).astype(BF16) * K["gmask"][n_levels]
        for lvl, n in enumerate(level_sizes):
            mid = _mid_cum(cum, n)
            up = K["gupb"][lvl] > 0.0
            f = jnp.exp((cum - mid) * K["gsgn"][lvl]).astype(BF16)
            zero = jnp.zeros_like(qb)
            ql = jnp.where(up, qb * f, zero)
            kl = jnp.where(up, zero, kb * f)
            sc = _dot_nt(ql, stack_kv(kl, sm128)).astype(BF16)
            p = p + (sc if n == C else sc * K["gmask"][lvl])
        a_cum = jnp.exp(cum)
        s_old = S_gla_ref[...]
        o = _dot(p, stack_kv(v, sm256)) + _dot((q * a_cum).astype(BF16), s_old.astype(BF16))
        ds = _dot_tn((k * jnp.exp(rem)).astype(BF16), v.astype(BF16)) * K["bdg"][...]
        a_col = a_cum.T[:, C - 1:C]
        S_gla_ref[...] = s_old * a_col + ds
        o_gla.append(o)

    cat = lambda parts: parts[0] if n_chunks == 1 else jnp.concatenate(parts, axis=0)
    hmid_ref[...] = _mix_out(x, cat(o_ssd), cat(o_ret), cat(o_gla), y_lru, pr["xs"],
                             _silu(cols(C_Z, 256)), _silu(cols(C_RET + 768, 256)), _silu(cols(C_GR, 256)),
                             W, K["mavg"][...])

    tail_ssd_ref[...] = pr["xbc_raw"][T - 8:, :]
    tail_lru_ref[...] = pr["lru_x"][T - 8:, :]
    hl_ref[...] = l_h[T - 8:, :]


def _feature_major(a, out_ref):
    nb = SAMPLE_B
    for t in range(SAMPLE_T):
        for fb in range(a.shape[1] // 128):
            out_ref[t, fb * 128:(fb + 1) * 128, :] = a[t * nb:(t + 1) * nb, fb * 128:(fb + 1) * 128].T


def _token_major(o_ref):
    f = o_ref.shape[1]
    return jnp.concatenate(
        [jnp.concatenate([o_ref[t, fb * 128:(fb + 1) * 128, :].T for fb in range(f // 128)], axis=1)
         for t in range(SAMPLE_T)], axis=0)


def _sample_pre_kernel(*refs):
    it = iter(refs)
    x_ref, cos_ref, sin_ref, cst_ssd_ref, cst_lru_ref, h0_ref = (next(it) for _ in range(6))
    W = {n: next(it) for n in WEIGHT_NAMES}
    (ssd_q_ref, ssd_k_ref, ssd_v_ref, ssd_a_ref, ret_q_ref, ret_k_ref, ret_v_ref,
     gla_q_ref, gla_k_ref, gla_v_ref, gla_g_ref, side_ref, nconv_ssd_ref, nconv_lru_ref, nh_ref) = (
         next(it) for _ in range(15))
    nb = SAMPLE_B

    u = _dot(_rms(x_ref[...], W["norm_mix"][...]).astype(BF16), W["w_in"][...])
    cols = lambda a, w: u[:, a:a + w]

    def conv(x, w_ref, cst_ref):
        ext = jnp.concatenate([cst_ref[0], cst_ref[1], cst_ref[2], x], axis=0)
        y = w_ref[4:5, :] + x * w_ref[3:4, :]
        for j in range(3):
            y = y + ext[j * nb:(j + SAMPLE_T) * nb, :] * w_ref[j:j + 1, :]
        return y

    pr = _prep(cols, lambda v: conv(v, W["cw_ssd"], cst_ssd_ref), lambda v: conv(v, W["cw_lru"], cst_lru_ref),
               cos_ref[...], sin_ref[...], W)
    nconv_ssd_ref[...] = pr["xbc_raw"][nb:, :].reshape(3, nb, SSD_CONV_DIM)
    nconv_lru_ref[...] = pr["lru_x"][nb:, :].reshape(3, nb, GROUP_WIDTH)

    h = h0_ref[...]
    hs = []
    for t in range(SAMPLE_T):
        h = pr["l_a"][t * nb:(t + 1) * nb, :] * h + pr["l_u"][t * nb:(t + 1) * nb, :]
        hs.append(h)
    nh_ref[...] = h
    y_lru = jnp.concatenate(hs, axis=0) * _gelu_tanh(cols(C_LY, 256))

    _feature_major(pr["ssd_q"], ssd_q_ref)
    _feature_major(pr["ssd_k"], ssd_k_ref)
    _feature_major(pr["ssd_v"], ssd_v_ref)
    _feature_major(jnp.exp(pr["ssd_la"]), ssd_a_ref)
    _feature_major(pr["rq"], ret_q_ref)
    _feature_major(pr["rk"], ret_k_ref)
    _feature_major(cols(C_RET + 512, 256), ret_v_ref)
    _feature_major(pr["gq"], gla_q_ref)
    _feature_major(cols(C_GK, 128), gla_k_ref)
    _feature_major(cols(C_GV, 256), gla_v_ref)
    _feature_major(jnp.exp(pr["gla_lg"]), gla_g_ref)
    side_ref[:, 0:256] = pr["xs"]
    side_ref[:, 256:512] = _silu(cols(C_Z, 256))
    side_ref[:, 512:768] = _silu(cols(C_RET + 768, 256))
    side_ref[:, 768:1024] = _silu(cols(C_GR, 256))
    side_ref[:, 1024:1280] = y_lru


def _sample_state_kernel(ssd_q_ref, ssd_k_ref, ssd_v_ref, ssd_a_ref, ret_q_ref, ret_k_ref, ret_v_ref, gam_ref,
                         gla_q_ref, gla_k_ref, gla_v_ref, gla_g_ref, s_ssd_ref, s_ret_ref, s_gla_ref,
                         *rest):
    o_ssd_ref, o_ret_ref, o_gla_ref, n_ssd_ref, n_ret_ref, n_gla_ref = rest[-6:]

    def run(q_ref, k_ref, v_ref, decay, s_ref, n_ref, o_ref):
        def body(d, acc):
            s = s_ref[d]
            new = []
            for t in range(SAMPLE_T):
                s = decay(t, d) * s + k_ref[t, pl.ds(d, 1), :] * v_ref[t]
                new.append(acc[t] + q_ref[t, pl.ds(d, 1), :] * s)
            n_ref[d] = s
            return tuple(new)

        zero = jnp.zeros((HEAD_DIM, SAMPLE_B), F32)
        acc = lax.fori_loop(0, s_ref.shape[0], body, (zero,) * SAMPLE_T, unroll=2)
        for t in range(SAMPLE_T):
            o_ref[t] = acc[t]

    run(ssd_q_ref, ssd_k_ref, ssd_v_ref, lambda t, d: ssd_a_ref[t, 0:1, :], s_ssd_ref, n_ssd_ref, o_ssd_ref)
    run(ret_q_ref, ret_k_ref, ret_v_ref, lambda t, d: gam_ref[0:1, :], s_ret_ref, n_ret_ref, o_ret_ref)
    run(gla_q_ref, gla_k_ref, gla_v_ref, lambda t, d: gla_g_ref[t, pl.ds(d, 1), :], s_gla_ref, n_gla_ref,
        o_gla_ref)


def _sample_post_kernel(x_ref, side_ref, o_ssd_ref, o_ret_ref, o_gla_ref, vec_ref, wout_ref, mavg_ref,
                        nf_ref, wg_ref, wu_ref, wd_ref, nfin_ref, out_ref, *, final_norm):
    side = lambda i: side_ref[:, i * 256:(i + 1) * 256]
    W = dict(vec=vec_ref, w_out=wout_ref)
    hmid = _mix_out(x_ref[...], _token_major(o_ssd_ref), _token_major(o_ret_ref), _token_major(o_gla_ref),
                    side(4), side(0), side(1), side(2), side(3), W, mavg_ref[...])
    out_ref[...] = _ffn_math(hmid, nf_ref, wg_ref, wu_ref, wd_ref, nfin_ref, final_norm)


def _layer_spec(a, layer, grid_rank):
    idx = (layer,) + (0,) * (a.ndim - 1)
    imap = (lambda i: idx) if grid_rank == 1 else (lambda b, t: idx)
    return pl.BlockSpec((None,) + a.shape[1:], imap, pipeline_mode=pl.Buffered(1))


def _full_spec(a, grid_rank):
    zeros = (0,) * a.ndim
    return pl.BlockSpec(a.shape, (lambda i: zeros) if grid_rank == 1 else (lambda b, t: zeros))


def _ffn(h2d, layer, fw, norm_final, final_norm, tm):
    rows = h2d.shape[0]
    weights = [fw[n] for n in ("norm_ffn", "wg", "wu", "wd")]
    return pl.pallas_call(
        functools.partial(_ffn_kernel, final_norm=final_norm),
        out_shape=jax.ShapeDtypeStruct(h2d.shape, F32),
        grid=(rows // tm,),
        in_specs=[pl.BlockSpec((tm, D_MODEL), lambda i: (i, 0))]
                 + [_layer_spec(a, layer, 1) for a in weights] + [_full_spec(norm_final, 1)],
        out_specs=pl.BlockSpec((tm, D_MODEL), lambda i: (i, 0)),
        compiler_params=pltpu.CompilerParams(
            dimension_semantics=("arbitrary",), vmem_limit_bytes=VMEM_LIMIT_BYTES),
        name="ffn",
    )(h2d, *weights, norm_final)


def _repack_kernel(w_ref, o_ref, *, depth):
    lane = lax.broadcasted_iota(jnp.int32, (D_MODEL, 128), 1)
    for l in range(depth):
        block = lambda src: w_ref[src:src + 128, l, :].T
        put = lambda dst, val: o_ref.__setitem__((l, slice(None), slice(dst, dst + 128)), val.astype(BF16))
        for dst, src, n in ((C_Z, 0, 768), (C_RET, S_RET, 1024), (C_GQ, S_GLA, 512),
                            (C_GR, S_GR, 256), (C_LY, S_LRU, 512)):
            for off in range(0, n, 128):
                put(dst + off, block(src + off))
        dtb = block(S_DT)
        for half in range(2):
            first = jnp.broadcast_to(dtb[:, 2 * half:2 * half + 1], (D_MODEL, 128))
            second = jnp.broadcast_to(dtb[:, 2 * half + 1:2 * half + 2], (D_MODEL, 128))
            put(C_DT + 128 * half, jnp.where(lane < HEAD_DIM, first, second))
        base = (S_GLR // 128) * 128
        win = pltpu.roll(block(base), 128 - (S_GLR - base), axis=1)
        put(C_MISC, jnp.where(lane < GLA_GATE_RANK, win, 0.0))


def _repack_w_in(w_in):
    depth = w_in.shape[0]
    w_cols = jnp.transpose(w_in, (2, 0, 1))
    return pl.pallas_call(
        functools.partial(_repack_kernel, depth=depth),
        out_shape=jax.ShapeDtypeStruct((depth, D_MODEL, IN_PACKED), BF16),
        grid=(1,),
        in_specs=[_full_spec(w_cols, 1)],
        out_specs=pl.BlockSpec((depth, D_MODEL, IN_PACKED), lambda i: (0, 0, 0)),
        compiler_params=pltpu.CompilerParams(
            dimension_semantics=("arbitrary",), vmem_limit_bytes=VMEM_LIMIT_BYTES),
        name="repack_w_in",
    )(w_cols)


def _pack_weights(p):
    depth = p["w_in"].shape[0]
    pad_rows = lambda a, n: jnp.concatenate(
        [a, jnp.zeros((depth, n - a.shape[1], a.shape[2]), F32)], axis=1)
    rep = lambda a: jnp.repeat(a, HEAD_DIM, axis=1)
    vec = jnp.stack([rep(p["ssd_dt_bias"]), rep(p["ssd_a_log"]), rep(p["ssd_d"]),
                     p["ssd_norm"], p["ret_norm"], p["gla_norm"],
                     p["lru_b_a"], p["lru_b_x"], p["lru_lambda"]], axis=1)
    eye = jnp.eye(N_HEADS, dtype=F32)
    bd = lambda a: jnp.einsum("lhij,hg->lhigj", a, eye).reshape(depth, GROUP_WIDTH, GROUP_WIDTH).astype(BF16)
    return dict(
        norm_mix=p["norm_mix"][:, None, :], w_in=_repack_w_in(p["w_in"]),
        cw_ssd=pad_rows(jnp.concatenate([p["ssd_conv_w"], p["ssd_conv_b"][:, None, :]], axis=1), 8),
        cw_lru=pad_rows(jnp.concatenate([p["lru_conv_w"], p["lru_conv_b"][:, None, :]], axis=1), 8),
        vec=pad_rows(vec, 16), bg=pad_rows(p["gla_b_gate"][:, None, :], 8),
        w2=pad_rows(p["gla_w_gate2"], 128).astype(BF16),
        wa=bd(p["lru_w_a"]), wx=bd(p["lru_w_x"]), w_out=p["w_out"].astype(BF16))


def _rope_tables(pos):
    half = HEAD_DIM // 2
    inv = ROPE_BASE ** (-jnp.arange(half, dtype=F32) / half)
    ang = pos.astype(F32)[:, None] * inv[None, :]
    cos, sin = jnp.cos(ang), jnp.sin(ang)
    cos = jnp.tile(jnp.concatenate([cos, cos], axis=1), (1, N_HEADS))
    sin = jnp.tile(jnp.concatenate([-sin, sin], axis=1), (1, N_HEADS))
    return cos, sin


def _prompt_mixer(h, cos, sin, layer, wts, consts, tq):
    b, l, _ = h.shape
    weights = [wts[n] for n in WEIGHT_NAMES]
    cvals = [consts[n] for n in CONST_NAMES]
    row = lambda w: pl.BlockSpec((None, tq, w), lambda bi, ti: (bi, ti, 0))
    per_seq = lambda r, w: pl.BlockSpec((None, r, w), lambda bi, ti: (bi, 0, 0))
    out_shapes = (jax.ShapeDtypeStruct((b, l, D_MODEL), F32),
                  jax.ShapeDtypeStruct((b, 8, SSD_CONV_DIM), F32),
                  jax.ShapeDtypeStruct((b, 8, GROUP_WIDTH), F32),
                  jax.ShapeDtypeStruct((b, 8, GROUP_WIDTH), F32),
                  jax.ShapeDtypeStruct((b, 256, 256), F32),
                  jax.ShapeDtypeStruct((b, 256, 256), F32),
                  jax.ShapeDtypeStruct((b, 128, 256), F32))
    return pl.pallas_call(
        functools.partial(_mixer_kernel, n_chunks=tq // CHUNK, level_sizes=_level_sizes(CHUNK)),
        out_shape=out_shapes,
        grid=(b, l // tq),
        in_specs=[row(D_MODEL),
                  pl.BlockSpec((tq, GROUP_WIDTH), lambda bi, ti: (ti, 0)),
                  pl.BlockSpec((tq, GROUP_WIDTH), lambda bi, ti: (ti, 0))]
                 + [_layer_spec(a, layer, 2) for a in weights] + [_full_spec(a, 2) for a in cvals],
        out_specs=(row(D_MODEL), per_seq(8, SSD_CONV_DIM), per_seq(8, GROUP_WIDTH),
                   per_seq(8, GROUP_WIDTH), per_seq(256, 256), per_seq(256, 256), per_seq(128, 256)),
        scratch_shapes=[pltpu.VMEM((8 + tq, SSD_CONV_DIM), F32), pltpu.VMEM((8 + tq, GROUP_WIDTH), F32)],
        compiler_params=pltpu.CompilerParams(
            dimension_semantics=("arbitrary", "arbitrary"), vmem_limit_bytes=VMEM_LIMIT_BYTES),
        name="prompt_mixer",
    )(h, cos, sin, *weights, *cvals)


def _sample_pre(hs, cos, sin, cst_ssd, cst_lru, h0, layer, wts):
    t, b = SAMPLE_T, SAMPLE_B
    weights = [wts[n] for n in WEIGHT_NAMES]
    fm = lambda f: jax.ShapeDtypeStruct((t, f, b), F32)
    out_shapes = (fm(256),) * 7 + (fm(128), fm(128), fm(256), fm(128)) + (
        jax.ShapeDtypeStruct((t * b, 1280), F32),
        jax.ShapeDtypeStruct((3, b, SSD_CONV_DIM), F32), jax.ShapeDtypeStruct((3, b, GROUP_WIDTH), F32),
        jax.ShapeDtypeStruct((b, GROUP_WIDTH), F32))
    lay = lambda a: pl.BlockSpec((None,) + a.shape[1:], lambda i: (layer,) + (0,) * (a.ndim - 1))
    return pl.pallas_call(
        _sample_pre_kernel,
        out_shape=out_shapes,
        grid=(1,),
        in_specs=[_full_spec(hs, 1), _full_spec(cos, 1), _full_spec(sin, 1), lay(cst_ssd), lay(cst_lru), lay(h0)]
                 + [_layer_spec(a, layer, 1) for a in weights],
        out_specs=tuple(pl.BlockSpec(o.shape, lambda i, n=len(o.shape): (0,) * n) for o in out_shapes),
        compiler_params=pltpu.CompilerParams(
            dimension_semantics=("arbitrary",), vmem_limit_bytes=VMEM_LIMIT_BYTES),
        name="sample_pre",
    )(hs, cos, sin, cst_ssd, cst_lru, h0, *weights)


def _sample_state(ops, gam, states, new_states, layer):
    t, b = SAMPLE_T, SAMPLE_B
    ssd_q, ssd_k, ssd_v, ssd_a, ret_q, ret_k, ret_v, gla_q, gla_k, gla_v, gla_g = ops
    aliased = list(new_states)
    blk = lambda f: pl.BlockSpec((t, f // N_HEADS, b), lambda h: (0, h, 0))
    st = lambda dk: pl.BlockSpec((None, None, dk, HEAD_DIM, b), lambda h: (layer, h, 0, 0, 0))
    operands = [ssd_q, ssd_k, ssd_v, ssd_a, ret_q, ret_k, ret_v, gam, gla_q, gla_k, gla_v, gla_g, *states]
    in_specs = ([blk(256)] * 7 + [pl.BlockSpec((None, 8, b), lambda h: (h, 0, 0))]
                + [blk(128), blk(128), blk(256), blk(128), st(64), st(64), st(32)]
                + [pl.BlockSpec(memory_space=pl.ANY) for _ in aliased])
    out_shapes = (jax.ShapeDtypeStruct((t, 256, b), F32),) * 3 + tuple(
        jax.ShapeDtypeStruct(s.shape, F32) for s in states)
    return pl.pallas_call(
        _sample_state_kernel,
        out_shape=out_shapes,
        grid=(N_HEADS,),
        in_specs=in_specs,
        out_specs=(blk(256), blk(256), blk(256), st(64), st(64), st(32)),
        input_output_aliases={len(operands) + k: 3 + k for k in range(len(aliased))},
        compiler_params=pltpu.CompilerParams(
            dimension_semantics=("arbitrary",), vmem_limit_bytes=VMEM_LIMIT_BYTES),
        name="sample_state",
    )(*operands, *aliased)


def _sample_post(hs, side, o_ssd, o_ret, o_gla, layer, wts, mavg, fw, norm_final, final_norm):
    lay = [wts["vec"], wts["w_out"]]
    ffn_w = [fw[n] for n in ("norm_ffn", "wg", "wu", "wd")]
    acts = [hs, side, o_ssd, o_ret, o_gla]
    return pl.pallas_call(
        functools.partial(_sample_post_kernel, final_norm=final_norm),
        out_shape=jax.ShapeDtypeStruct(hs.shape, F32),
        grid=(1,),
        in_specs=[_full_spec(a, 1) for a in acts] + [_layer_spec(a, layer, 1) for a in lay]
                 + [_full_spec(mavg, 1)] + [_layer_spec(a, layer, 1) for a in ffn_w]
                 + [_full_spec(norm_final, 1)],
        out_specs=_full_spec(hs, 1),
        compiler_params=pltpu.CompilerParams(
            dimension_semantics=("arbitrary",), vmem_limit_bytes=VMEM_LIMIT_BYTES),
        name="sample_post",
    )(*acts, *lay, mavg, *ffn_w, norm_final)


def _diag_blocks(s, dk):
    return jnp.stack([s[:, h * dk:(h + 1) * dk, h * HEAD_DIM:(h + 1) * HEAD_DIM]
                      for h in range(N_HEADS)], axis=1)


def kernel(x_prompt, x_sample, state_ssd_conv, state_ssd, state_ret, state_gla, state_lru_conv, state_lru,
           norm_mix, w_in, ssd_conv_w, ssd_conv_b, ssd_dt_bias, ssd_a_log, ssd_d, ssd_norm, ret_norm,
           gla_w_gate2, gla_b_gate, gla_norm, lru_conv_w, lru_conv_b, lru_w_a, lru_b_a, lru_w_x, lru_b_x,
           lru_lambda, w_out, norm_ffn, w_gate, w_up, w_down, norm_final):
    params = dict(norm_mix=norm_mix, w_in=w_in, ssd_conv_w=ssd_conv_w, ssd_conv_b=ssd_conv_b,
                  ssd_dt_bias=ssd_dt_bias, ssd_a_log=ssd_a_log, ssd_d=ssd_d, ssd_norm=ssd_norm,
                  ret_norm=ret_norm, gla_w_gate2=gla_w_gate2, gla_b_gate=gla_b_gate, gla_norm=gla_norm,
                  lru_conv_w=lru_conv_w, lru_conv_b=lru_conv_b, lru_w_a=lru_w_a, lru_b_a=lru_b_a,
                  lru_w_x=lru_w_x, lru_b_x=lru_b_x, lru_lambda=lru_lambda, w_out=w_out)
    depth = w_in.shape[0]
    bp, lp, _ = x_prompt.shape
    bs, ls, _ = x_sample.shape
    assert (bs, ls) == (SAMPLE_B, SAMPLE_T) and lp % CHUNK == 0
    tq = 512 if lp % 512 == 0 else CHUNK
    tm_p = 512 if (bp * lp) % 512 == 0 else CHUNK

    wts = _pack_weights(params)
    fw = dict(norm_ffn=norm_ffn[:, None, :], wg=w_gate.astype(BF16), wu=w_up.astype(BF16),
              wd=w_down.astype(BF16))
    consts = _const_arrays()
    cos_p, sin_p = _rope_tables(jnp.arange(lp, dtype=jnp.int32))
    cos_s, sin_s = _rope_tables(PAST_LEN + jnp.arange(ls, dtype=jnp.int32))
    cos_s, sin_s = jnp.repeat(cos_s, bs, axis=0), jnp.repeat(sin_s, bs, axis=0)
    nfin = norm_final[None]
    gam = jnp.broadcast_to(jnp.exp(_ret_log_gamma())[:, None, None], (N_HEADS, 8, bs))
    cst_ssd = jnp.transpose(state_ssd_conv, (0, 2, 1, 3))
    cst_lru = jnp.transpose(state_lru_conv, (0, 2, 1, 3))
    states_s = tuple(jnp.transpose(a, (0, 2, 3, 4, 1)) for a in (state_ssd, state_ret, state_gla))

    hp = x_prompt
    hs = jnp.transpose(x_sample, (1, 0, 2)).reshape(ls * bs, D_MODEL)
    new_p = [[] for _ in range(6)]
    new_s = [[] for _ in range(3)]
    new_states_s = tuple(jnp.zeros(a.shape, F32) for a in states_s)
    for i in range(depth):
        last = i == depth - 1

        hmid, tail_ssd, tail_lru, hl, s_ssd, s_ret, s_gla = _prompt_mixer(
            hp, cos_p, sin_p, i, wts, consts, tq)
        hp = _ffn(hmid.reshape(bp * lp, D_MODEL), i, fw, nfin, last, tm_p).reshape(bp, lp, D_MODEL)
        for j, a in enumerate((tail_ssd[:, 5:], _diag_blocks(s_ssd, 64), _diag_blocks(s_ret, 64),
                               _diag_blocks(s_gla, 32), tail_lru[:, 5:], hl[:, 7])):
            new_p[j].append(a)

        *ops, side, nconv_ssd, nconv_lru, nh = _sample_pre(
            hs, cos_s, sin_s, cst_ssd, cst_lru, state_lru, i, wts)
        o_ssd, o_ret, o_gla, *new_states_s = _sample_state(ops, gam, states_s, new_states_s, i)
        hs = _sample_post(hs, side, o_ssd, o_ret, o_gla, i, wts, consts["mavg"], fw, nfin, last)
        for j, a in enumerate((nconv_ssd, nconv_lru, nh)):
            new_s[j].append(a)

    outs_p = [jnp.stack(a, axis=0) for a in new_p]
    s_conv, s_lru_conv, s_lru = [jnp.stack(a, axis=0) for a in new_s]
    n_ssd, n_ret, n_gla = (jnp.transpose(a, (0, 4, 1, 2, 3)) for a in new_states_s)
    y_sample = jnp.transpose(hs.reshape(ls, bs, D_MODEL), (1, 0, 2))
    return (hp, y_sample, *outs_p,
            jnp.transpose(s_conv, (0, 2, 1, 3)), n_ssd, n_ret, n_gla,
            jnp.transpose(s_lru_conv, (0, 2, 1, 3)), s_lru)
```

```python
import functools
import math

import numpy as np
import jax
import jax.numpy as jnp
from jax import lax
from jax.experimental import pallas as pl
from jax.experimental.pallas import tpu as pltpu

F32 = jnp.float32
BF16 = jnp.bfloat16

D_MODEL = 1024
HEAD_DIM = 64
N_HEADS = 4
GROUP_WIDTH = 256
EPS = 1e-6
SSD_CONV_DIM = 512
GLA_KEY_WIDTH = 128
GLA_DK = 32
GLA_GATE_RANK = 16
GLA_GATE_TEMP = 16.0
LRU_C = 8.0
ROPE_BASE = 10000.0
PAST_LEN = 16384
IN_DIM = 3092

C_Z, C_XBC, C_DT, C_RET, C_GQ, C_GK, C_GV, C_GR, C_LY, C_LX, C_MISC, IN_PACKED = (
    0, 256, 768, 1024, 2048, 2176, 2304, 2560, 2816, 3072, 3328, 3456)
S_DT, S_RET, S_GLA, S_GLR, S_GR, S_LRU = 768, 772, 1796, 2308, 2324, 2580

CHUNK = 128
PROJ_ROWS, PROJ_COLS = 128, 256
FFN_COLS = 256
SAMPLE_B = 128
SAMPLE_T = 4
VMEM_LIMIT_BYTES = 56 * 1024 * 1024


def _level_sizes(seg):
    sizes = []
    n = seg
    while n >= 2:
        sizes.append(n)
        n //= 2
    return tuple(sizes)


@functools.lru_cache(maxsize=None)
def _chunk_constants():
    c = CHUNK
    i = np.arange(c)[:, None]
    j = np.arange(c)[None, :]
    up_levels, m_levels = [], []
    for n in _level_sizes(c):
        up_levels.append(np.broadcast_to((i % n) >= n // 2, (c, GLA_KEY_WIDTH)))
        m_levels.append(np.tile((i // n) == (j // n), (1, N_HEADS)))
    m_levels.append(np.tile(i == j, (1, N_HEADS)))
    rows = np.arange(N_HEADS * c)[:, None]
    hh = np.arange(GROUP_WIDTH) // HEAD_DIM
    return dict(
        tri=(j <= i).astype(np.float32),
        gmask=np.stack(m_levels).astype(np.float32),
        gup=np.stack(up_levels).astype(np.float32),
        gsgn=np.where(np.stack(up_levels), 1.0, -1.0).astype(np.float32),
        sm256=((rows // c) == hh[None, :]).astype(np.float32),
        sm128=((rows // c) == (np.arange(GLA_KEY_WIDTH)[None, :] // GLA_DK)).astype(np.float32),
        bdm=(hh[:, None] == hh[None, :]).astype(np.float32),
        bdg=((np.arange(GLA_KEY_WIDTH) // GLA_DK)[:, None] == hh[None, :]).astype(np.float32))


def _ret_log_gamma():
    return jnp.log(1.0 - 2.0 ** (-5.0 - jnp.arange(N_HEADS, dtype=F32)))


def _const_arrays():
    k = _chunk_constants()
    c = CHUNK
    log_gamma = _ret_log_gamma()
    lg256 = jnp.repeat(log_gamma, HEAD_DIM)[None, :]
    pos = jnp.arange(c, dtype=F32)[:, None]
    dpos = pos - pos.T
    tri = jnp.asarray(k["tri"])
    dret = jnp.concatenate(
        [jnp.where(tri > 0, jnp.exp(dpos * log_gamma[h]), 0.0) for h in range(N_HEADS)], axis=1)
    return dict(
        tri=jnp.asarray(k["tri"], BF16),
        cmask=tri, gmask=jnp.asarray(k["gmask"], BF16), gupb=jnp.asarray(k["gup"], BF16),
        gsgn=jnp.asarray(k["gsgn"]),
        retq=jnp.exp((pos + 1.0) * lg256), retk=jnp.exp((c - 1.0 - pos) * lg256),
        rets=jnp.broadcast_to(jnp.exp(float(c) * lg256), (8, GROUP_WIDTH)), dret=dret,
        sm256=jnp.asarray(k["sm256"], BF16), sm128=jnp.asarray(k["sm128"], BF16),
        bdm=jnp.asarray(k["bdm"]), bdg=jnp.asarray(k["bdg"]),
        mavg=jnp.asarray(k["bdm"] / HEAD_DIM, BF16))


CONST_NAMES = ("tri", "cmask", "gmask", "gupb", "gsgn", "retq", "retk", "rets", "dret",
               "sm256", "sm128", "bdm", "bdg", "mavg")
WEIGHT_NAMES = ("norm_mix", "w_in", "cw_ssd", "cw_lru", "vec", "bg", "w2", "wa", "wx", "w_out")

V_DTB, V_ALOG, V_SSDD, V_SSDN, V_RETN, V_GLAN, V_LBA, V_LBX, V_LLAM = range(9)


def _dot(a, b):
    return jnp.dot(a, b, preferred_element_type=F32)


def _dot_nt(a, b):
    return lax.dot_general(a, b, (((1,), (1,)), ((), ())), preferred_element_type=F32)


def _dot_tn(a, b):
    return lax.dot_general(a, b, (((0,), (0,)), ((), ())), preferred_element_type=F32)


def _split3(x):
    hi = x.astype(BF16)
    r1 = x - hi.astype(F32)
    mid = r1.astype(BF16)
    lo = (r1 - mid.astype(F32)).astype(BF16)
    return hi, mid, lo


def _dot3(w01, x):
    hi, mid, lo = _split3(x)
    return _dot(w01, hi) + _dot(w01, mid) + _dot(w01, lo)


def _sigmoid(x):
    return 1.0 / (1.0 + jnp.exp(-x))


def _silu(x):
    return x * _sigmoid(x)


def _log1pexp_neg_abs(x):
    return jnp.log(1.0 + jnp.exp(-jnp.abs(x)))


def _softplus(x):
    return jnp.maximum(x, 0.0) + _log1pexp_neg_abs(x)


def _gelu_tanh(x):
    return 0.5 * x * (1.0 + jnp.tanh(math.sqrt(2.0 / math.pi) * (x + 0.044715 * (x * x * x))))


def _rms(x, w):
    return x * lax.rsqrt(jnp.mean(x * x, axis=-1, keepdims=True) + EPS) * w


def _stack4(x):
    return jnp.concatenate([x, x, x, x], axis=0)


def _causal_conv_carried(x, w_ref, buf_ref):
    rows = x.shape[0]
    buf_ref[8:8 + rows, :] = x
    y = w_ref[4:5, :] + x * w_ref[3:4, :]
    for k in (1, 2, 3):
        y = y + buf_ref[8 - k:8 - k + rows, :] * w_ref[3 - k:4 - k, :]
    buf_ref[0:8, :] = x[rows - 8:, :]
    return y


def _swap_halves(x):
    w = x.shape[1]
    lane = lax.broadcasted_iota(jnp.int32, x.shape, 1)
    return jnp.where((lane & 32) == 0, pltpu.roll(x, w - 32, axis=1), pltpu.roll(x, 32, axis=1))


def _expand_groups(x):
    lane = lax.broadcasted_iota(jnp.int32, x.shape, 1)
    r = pltpu.roll(x, 64, axis=1)
    first = lane < 64
    return jnp.concatenate([jnp.where(first, x, r), jnp.where(first, r, x)], axis=1)


def _head_cols(cum):
    cols = []
    for half in range(2):
        x = cum[:, half * 128:(half + 1) * 128]
        lane = lax.broadcasted_iota(jnp.int32, x.shape, 1)
        r = pltpu.roll(x, 64, axis=1)
        first = lane < 64
        cols.append(jnp.where(first, x, r))
        cols.append(jnp.where(first, r, x))
    return cols


def _mid_cum(cum, n):
    rows, w = cum.shape
    half = n // 2
    if n >= 16:
        blocks = cum.reshape(rows // n, n, w)
        return jnp.broadcast_to(blocks[:, half - 1:half, :], blocks.shape).reshape(rows, w)
    pos = lax.broadcasted_iota(jnp.int32, cum.shape, 0) & (n - 1)
    out = cum
    for p in range(n):
        delta = half - 1 - p
        if delta != 0:
            out = jnp.where(pos == p, pltpu.roll(cum, (-delta) % rows, axis=0), out)
    return out


def _scan_rows(a, u, h_in):
    rows, w = a.shape
    pos = lax.broadcasted_iota(jnp.int32, a.shape, 0) & 7
    for s in (1, 2, 4):
        keep = pos >= s
        a_s = jnp.where(keep, pltpu.roll(a, s, axis=0), 1.0)
        u_s = jnp.where(keep, pltpu.roll(u, s, axis=0), 0.0)
        u = a * u_s + u
        a = a * a_s
    groups = rows // 8
    a3, u3 = a.reshape(groups, 8, w), u.reshape(groups, 8, w)
    carries = []
    h = h_in
    for g in range(groups):
        carries.append(h)
        h = a3[g, 7:8, :] * h + u3[g, 7:8, :]
    carry = jnp.broadcast_to(jnp.stack(carries, axis=0), (groups, 8, w))
    return (u3 + a3 * carry).reshape(rows, w)


def _ffn_math(x, nf_ref, wg_ref, wu_ref, wd_ref, nfin_ref, final_norm):
    hn = _rms(x, nf_ref[...]).astype(BF16)
    d_ff = wg_ref.shape[1]
    acts = []
    for c0 in range(0, d_ff, FFN_COLS):
        c1 = min(c0 + FFN_COLS, d_ff)
        g = _dot(hn, wg_ref[:, c0:c1])
        up = _dot(hn, wu_ref[:, c0:c1])
        acts.append((_silu(g) * up).astype(BF16))
    y = x + _dot(jnp.concatenate(acts, axis=1), wd_ref[...])
    return _rms(y, nfin_ref[...]) if final_norm else y


def _ffn_kernel(x_ref, nf_ref, wg_ref, wu_ref, wd_ref, nfin_ref, o_ref, *, final_norm):
    o_ref[...] = _ffn_math(x_ref[...], nf_ref, wg_ref, wu_ref, wd_ref, nfin_ref, final_norm)


def _prep(cols, conv_ssd_fn, conv_lru_fn, cos, sin, W):
    vec = W["vec"]
    xbc_raw, lru_x = cols(C_XBC, 512), cols(C_LX, 256)
    xbc = _silu(conv_ssd_fn(xbc_raw))
    xs = xbc[:, :256]
    dt = _softplus(cols(C_DT, 256) + vec[V_DTB:V_DTB + 1, :])
    ret_q, ret_k = cols(C_RET, 256), cols(C_RET + 256, 256)
    gate_x = _dot(cols(C_MISC, 128).astype(BF16), W["w2"][...]) + W["bg"][0:1, :]
    xc = conv_lru_fn(lru_x)
    xcb = xc.astype(BF16)
    lr = _sigmoid(_dot(xcb, W["wa"][...]) + vec[V_LBA:V_LBA + 1, :])
    li = _sigmoid(_dot(xcb, W["wx"][...]) + vec[V_LBX:V_LBX + 1, :])
    l_loga = -LRU_C * lr * _softplus(-vec[V_LLAM:V_LLAM + 1, :])
    l_a = jnp.exp(l_loga)
    return dict(
        xbc_raw=xbc_raw, lru_x=lru_x, xs=xs,
        ssd_k=_expand_groups(xbc[:, 256:384]), ssd_q=_expand_groups(xbc[:, 384:512]),
        ssd_la=-dt * jnp.exp(vec[V_ALOG:V_ALOG + 1, :]),
        ssd_v=xs * dt,
        rq=ret_q * cos + _swap_halves(ret_q) * sin,
        rk=(ret_k * cos + _swap_halves(ret_k) * sin) * (HEAD_DIM ** -0.5),
        gla_lg=(jnp.minimum(gate_x, 0.0) - _log1pexp_neg_abs(gate_x)) / GLA_GATE_TEMP,
        gq=cols(C_GQ, 128) * (GLA_DK ** -0.5),
        l_a=l_a,
        l_u=jnp.sqrt(-jnp.tanh(l_loga) * (l_a * l_a + 1.0)) * (li * xc))


def _mix_out(x, o_ssd, o_ret, o_gla, y_lru, xs, g_ssd, g_ret, g_gla, W, mavg):
    vec = W["vec"]
    y_ssd = _rms((o_ssd + xs * vec[V_SSDD:V_SSDD + 1, :]) * g_ssd, vec[V_SSDN:V_SSDN + 1, :])
    mu = _dot(o_ret.astype(BF16), mavg)
    d = o_ret - mu
    var = _dot((d * d).astype(BF16), mavg)
    y_ret = d * lax.rsqrt(var + EPS) * vec[V_RETN:V_RETN + 1, :] * g_ret
    ms = _dot((o_gla * o_gla).astype(BF16), mavg)
    y_gla = o_gla * lax.rsqrt(ms + EPS) * vec[V_GLAN:V_GLAN + 1, :] * g_gla
    ymix = jnp.concatenate([y_ssd, y_ret, y_gla, y_lru], axis=1).astype(BF16)
    return x + _dot(ymix, W["w_out"][...])


def _mixer_kernel(*refs, n_chunks, level_sizes):
    it = iter(refs)
    x_ref, cos_ref, sin_ref = next(it), next(it), next(it)
    W = {n: next(it) for n in WEIGHT_NAMES}
    K = {n: next(it) for n in CONST_NAMES}
    hmid_ref = next(it)
    tail_ssd_ref, tail_lru_ref, hl_ref = next(it), next(it), next(it)
    S_ssd_ref, S_ret_ref, S_gla_ref = next(it), next(it), next(it)
    cbuf_ssd_ref, cbuf_lru_ref = next(it), next(it)

    C = CHUNK
    n_levels = len(level_sizes)

    @pl.when(pl.program_id(1) == 0)
    def _():
        cbuf_ssd_ref[0:8, :] = jnp.zeros((8, SSD_CONV_DIM), F32)
        cbuf_lru_ref[0:8, :] = jnp.zeros((8, GROUP_WIDTH), F32)
        hl_ref[...] = jnp.zeros_like(hl_ref)
        S_ssd_ref[...] = jnp.zeros_like(S_ssd_ref)
        S_ret_ref[...] = jnp.zeros_like(S_ret_ref)
        S_gla_ref[...] = jnp.zeros_like(S_gla_ref)

    x = x_ref[...]
    T = x.shape[0]
    xn = _rms(x, W["norm_mix"][...]).astype(BF16)
    u_blocks = {}

    def cols(a, w):
        parts = []
        for c0 in range(a, a + w, 128):
            if c0 not in u_blocks:
                base = (c0 // PROJ_COLS) * PROJ_COLS
                width = min(PROJ_COLS, IN_PACKED - base)
                wc = W["w_in"][:, base:base + width]
                full = jnp.concatenate(
                    [_dot(xn[r:r + PROJ_ROWS, :], wc) for r in range(0, T, PROJ_ROWS)], axis=0)
                for k in range(width // 128):
                    u_blocks[base + 128 * k] = full[:, 128 * k:128 * (k + 1)]
            parts.append(u_blocks[c0])
        return parts[0] if len(parts) == 1 else jnp.concatenate(parts, axis=1)

    pr = _prep(cols, lambda v: _causal_conv_carried(v, W["cw_ssd"], cbuf_ssd_ref),
               lambda v: _causal_conv_carried(v, W["cw_lru"], cbuf_lru_ref),
               cos_ref[...], sin_ref[...], W)
    ret_v, gla_k, gla_v = cols(C_RET + 512, 256), cols(C_GK, 128), cols(C_GV, 256)
    l_h = _scan_rows(pr["l_a"], pr["l_u"], hl_ref[7:8, :])
    y_lru = l_h * _gelu_tanh(cols(C_LY, 256))

    cmask = K["cmask"][...] > 0.0
    sm256 = K["sm256"][...]
    sm128 = K["sm128"][...]

    def stack_kv(a, mask):
        return _stack4(a.astype(BF16)) * mask

    def cum_and_rest(la):
        cum = _dot3(K["tri"][...], la)
        return cum, cum[C - 1:C, :] - cum

    o_ssd, o_ret, o_gla = [], [], []
    for c in range(n_chunks):
        sl = slice(c * C, (c + 1) * C)

        cum, rem = cum_and_rest(pr["ssd_la"][sl])
        q, k, v = pr["ssd_q"][sl], pr["ssd_k"][sl], pr["ssd_v"][sl]
        dec = jnp.concatenate(
            [jnp.exp(jnp.where(cmask, col - col.T, -jnp.inf)) for col in _head_cols(cum)], axis=1)
        p = _dot_nt(q.astype(BF16), stack_kv(k, sm256)) * dec
        a_cum = jnp.exp(cum)
        s_old = S_ssd_ref[...]
        o = _dot(p.astype(BF16), stack_kv(v, sm256)) + _dot((q * a_cum).astype(BF16), s_old.astype(BF16))
        ds = _dot_tn((k * jnp.exp(rem)).astype(BF16), v.astype(BF16)) * K["bdm"][...]
        S_ssd_ref[...] = s_old * a_cum[C - 1:C, :] + ds
        o_ssd.append(o)

        q, k, v = pr["rq"][sl], pr["rk"][sl], ret_v[sl]
        p = _dot_nt(q.astype(BF16), stack_kv(k, sm256)) * K["dret"][...]
        s_old = S_ret_ref[...]
        o = (_dot(p.astype(BF16), stack_kv(v, sm256))
             + _dot((q * K["retq"][...]).astype(BF16), s_old.astype(BF16)))
        ds = _dot_tn((k * K["retk"][...]).astype(BF16), v.astype(BF16)) * K["bdm"][...]
        S_ret_ref[...] = s_old * K["rets"][0:1, :] + ds
        o_ret.append(o)

        cum, rem = cum_and_rest(pr["gla_lg"][sl])
        q, k, v = pr["gq"][sl], gla_k[sl], gla_v[sl]
        qb, kb = q.astype(BF16), k.astype(BF16)
        p = _dot_nt(qb, stack_kv(kb, sm128)).astype(BF16) * K["gmask"][n_levels]
        for lvl, n in enumerate(level_sizes):
            mid = _mid_cum(cum, n)
            up = K["gupb"][lvl] > 0.0
            f = jnp.exp((cum - mid) * K["gsgn"][lvl]).astype(BF16)
            zero = jnp.zeros_like(qb)
            ql = jnp.where(up, qb * f, zero)
            kl = jnp.where(up, zero, kb * f)
            sc = _dot_nt(ql, stack_kv(kl, sm128)).astype(BF16)
            p = p + (sc if n == C else sc * K["gmask"][lvl])
        a_cum = jnp.exp(cum)
        s_old = S_gla_ref[...]
        o = _dot(p, stack_kv(v, sm256)) + _dot((q * a_cum).astype(BF16), s_old.astype(BF16))
        ds = _dot_tn((k * jnp.exp(rem)).astype(BF16), v.astype(BF16)) * K["bdg"][...]
        a_col = a_cum.T[:, C - 1:C]
        S_gla_ref[...] = s_old * a_col + ds
        o_gla.append(o)

    cat = lambda parts: parts[0] if n_chunks == 1 else jnp.concatenate(parts, axis=0)
    hmid_ref[...] = _mix_out(x, cat(o_ssd), cat(o_ret), cat(o_gla), y_lru, pr["xs"],
                             _silu(cols(C_Z, 256)), _silu(cols(C_RET + 768, 256)), _silu(cols(C_GR, 256)),
                             W, K["mavg"][...])

    tail_ssd_ref[...] = pr["xbc_raw"][T - 8:, :]
    tail_lru_ref[...] = pr["lru_x"][T - 8:, :]
    hl_ref[...] = l_h[T - 8:, :]


def _feature_major(a, out_ref):
    nb = SAMPLE_B
    for t in range(SAMPLE_T):
        for fb in range(a.shape[1] // 128):
            out_ref[t, fb * 128:(fb + 1) * 128, :] = a[t * nb:(t + 1) * nb, fb * 128:(fb + 1) * 128].T


def _token_major(o_ref):
    f = o_ref.shape[1]
    return jnp.concatenate(
        [jnp.concatenate([o_ref[t, fb * 128:(fb + 1) * 128, :].T for fb in range(f // 128)], axis=1)
         for t in range(SAMPLE_T)], axis=0)


def _sample_pre_kernel(*refs):
    it = iter(refs)
    x_ref, cos_ref, sin_ref, cst_ssd_ref, cst_lru_ref, h0_ref = (next(it) for _ in range(6))
    W = {n: next(it) for n in WEIGHT_NAMES}
    (ssd_q_ref, ssd_k_ref, ssd_v_ref, ssd_a_ref, ret_q_ref, ret_k_ref, ret_v_ref,
     gla_q_ref, gla_k_ref, gla_v_ref, gla_g_ref, side_ref, nconv_ssd_ref, nconv_lru_ref, nh_ref) = (
         next(it) for _ in range(15))
    nb = SAMPLE_B

    u = _dot(_rms(x_ref[...], W["norm_mix"][...]).astype(BF16), W["w_in"][...])
    cols = lambda a, w: u[:, a:a + w]

    def conv(x, w_ref, cst_ref):
        ext = jnp.concatenate([cst_ref[0], cst_ref[1], cst_ref[2], x], axis=0)
        y = w_ref[4:5, :] + x * w_ref[3:4, :]
        for j in range(3):
            y = y + ext[j * nb:(j + SAMPLE_T) * nb, :] * w_ref[j:j + 1, :]
        return y

    pr = _prep(cols, lambda v: conv(v, W["cw_ssd"], cst_ssd_ref), lambda v: conv(v, W["cw_lru"], cst_lru_ref),
               cos_ref[...], sin_ref[...], W)
    nconv_ssd_ref[...] = pr["xbc_raw"][nb:, :].reshape(3, nb, SSD_CONV_DIM)
    nconv_lru_ref[...] = pr["lru_x"][nb:, :].reshape(3, nb, GROUP_WIDTH)

    h = h0_ref[...]
    hs = []
    for t in range(SAMPLE_T):
        h = pr["l_a"][t * nb:(t + 1) * nb, :] * h + pr["l_u"][t * nb:(t + 1) * nb, :]
        hs.append(h)
    nh_ref[...] = h
    y_lru = jnp.concatenate(hs, axis=0) * _gelu_tanh(cols(C_LY, 256))

    _feature_major(pr["ssd_q"], ssd_q_ref)
    _feature_major(pr["ssd_k"], ssd_k_ref)
    _feature_major(pr["ssd_v"], ssd_v_ref)
    _feature_major(jnp.exp(pr["ssd_la"]), ssd_a_ref)
    _feature_major(pr["rq"], ret_q_ref)
    _feature_major(pr["rk"], ret_k_ref)
    _feature_major(cols(C_RET + 512, 256), ret_v_ref)
    _feature_major(pr["gq"], gla_q_ref)
    _feature_major(cols(C_GK, 128), gla_k_ref)
    _feature_major(cols(C_GV, 256), gla_v_ref)
    _feature_major(jnp.exp(pr["gla_lg"]), gla_g_ref)
    side_ref[:, 0:256] = pr["xs"]
    side_ref[:, 256:512] = _silu(cols(C_Z, 256))
    side_ref[:, 512:768] = _silu(cols(C_RET + 768, 256))
    side_ref[:, 768:1024] = _silu(cols(C_GR, 256))
    side_ref[:, 1024:1280] = y_lru


def _sample_state_kernel(ssd_q_ref, ssd_k_ref, ssd_v_ref, ssd_a_ref, ret_q_ref, ret_k_ref, ret_v_ref, gam_ref,
                         gla_q_ref, gla_k_ref, gla_v_ref, gla_g_ref, s_ssd_ref, s_ret_ref, s_gla_ref,
                         *rest, first_layer):
    o_ssd_ref, o_ret_ref, o_gla_ref, n_ssd_ref, n_ret_ref, n_gla_ref = rest[-6:]

    def run(q_ref, k_ref, v_ref, decay, s_ref, n_ref, o_ref):
        if first_layer:
            for l in range(1, n_ref.shape[0]):
                n_ref[l] = jnp.zeros(n_ref.shape[1:], F32)
            n_ref = n_ref.at[0]

        def body(d, acc):
            s = s_ref[d]
            new = []
            for t in range(SAMPLE_T):
                s = decay(t, d) * s + k_ref[t, pl.ds(d, 1), :] * v_ref[t]
                new.append(acc[t] + q_ref[t, pl.ds(d, 1), :] * s)
            n_ref[d] = s
            return tuple(new)

        zero = jnp.zeros((HEAD_DIM, SAMPLE_B), F32)
        acc = lax.fori_loop(0, s_ref.shape[0], body, (zero,) * SAMPLE_T, unroll=2)
        for t in range(SAMPLE_T):
            o_ref[t] = acc[t]

    run(ssd_q_ref, ssd_k_ref, ssd_v_ref, lambda t, d: ssd_a_ref[t, 0:1, :], s_ssd_ref, n_ssd_ref, o_ssd_ref)
    run(ret_q_ref, ret_k_ref, ret_v_ref, lambda t, d: gam_ref[0:1, :], s_ret_ref, n_ret_ref, o_ret_ref)
    run(gla_q_ref, gla_k_ref, gla_v_ref, lambda t, d: gla_g_ref[t, pl.ds(d, 1), :], s_gla_ref, n_gla_ref,
        o_gla_ref)


def _sample_post_kernel(x_ref, side_ref, o_ssd_ref, o_ret_ref, o_gla_ref, vec_ref, wout_ref, mavg_ref,
                        nf_ref, wg_ref, wu_ref, wd_ref, nfin_ref, out_ref, *, final_norm):
    side = lambda i: side_ref[:, i * 256:(i + 1) * 256]
    W = dict(vec=vec_ref, w_out=wout_ref)
    hmid = _mix_out(x_ref[...], _token_major(o_ssd_ref), _token_major(o_ret_ref), _token_major(o_gla_ref),
                    side(4), side(0), side(1), side(2), side(3), W, mavg_ref[...])
    out_ref[...] = _ffn_math(hmid, nf_ref, wg_ref, wu_ref, wd_ref, nfin_ref, final_norm)


def _layer_spec(a, layer, grid_rank):
    idx = (layer,) + (0,) * (a.ndim - 1)
    imap = (lambda i: idx) if grid_rank == 1 else (lambda b, t: idx)
    return pl.BlockSpec((None,) + a.shape[1:], imap, pipeline_mode=pl.Buffered(1))


def _full_spec(a, grid_rank):
    zeros = (0,) * a.ndim
    return pl.BlockSpec(a.shape, (lambda i: zeros) if grid_rank == 1 else (lambda b, t: zeros))


def _ffn(h2d, layer, fw, norm_final, final_norm, tm):
    rows = h2d.shape[0]
    weights = [fw[n] for n in ("norm_ffn", "wg", "wu", "wd")]
    return pl.pallas_call(
        functools.partial(_ffn_kernel, final_norm=final_norm),
        out_shape=jax.ShapeDtypeStruct(h2d.shape, F32),
        grid=(rows // tm,),
        in_specs=[pl.BlockSpec((tm, D_MODEL), lambda i: (i, 0))]
                 + [_layer_spec(a, layer, 1) for a in weights] + [_full_spec(norm_final, 1)],
        out_specs=pl.BlockSpec((tm, D_MODEL), lambda i: (i, 0)),
        compiler_params=pltpu.CompilerParams(
            dimension_semantics=("arbitrary",), vmem_limit_bytes=VMEM_LIMIT_BYTES),
        name="ffn",
    )(h2d, *weights, norm_final)


def _repack_kernel(w_ref, o_ref, *, depth):
    lane = lax.broadcasted_iota(jnp.int32, (D_MODEL, 128), 1)
    for l in range(depth):
        block = lambda src: w_ref[src:src + 128, l, :].T
        put = lambda dst, val: o_ref.__setitem__((l, slice(None), slice(dst, dst + 128)), val.astype(BF16))
        for dst, src, n in ((C_Z, 0, 768), (C_RET, S_RET, 1024), (C_GQ, S_GLA, 512),
                            (C_GR, S_GR, 256), (C_LY, S_LRU, 512)):
            for off in range(0, n, 128):
                put(dst + off, block(src + off))
        dtb = block(S_DT)
        for half in range(2):
            first = jnp.broadcast_to(dtb[:, 2 * half:2 * half + 1], (D_MODEL, 128))
            second = jnp.broadcast_to(dtb[:, 2 * half + 1:2 * half + 2], (D_MODEL, 128))
            put(C_DT + 128 * half, jnp.where(lane < HEAD_DIM, first, second))
        base = (S_GLR // 128) * 128
        win = pltpu.roll(block(base), 128 - (S_GLR - base), axis=1)
        put(C_MISC, jnp.where(lane < GLA_GATE_RANK, win, 0.0))


def _repack_w_in(w_in):
    depth = w_in.shape[0]
    w_cols = jnp.transpose(w_in, (2, 0, 1))
    return pl.pallas_call(
        functools.partial(_repack_kernel, depth=depth),
        out_shape=jax.ShapeDtypeStruct((depth, D_MODEL, IN_PACKED), BF16),
        grid=(1,),
        in_specs=[_full_spec(w_cols, 1)],
        out_specs=pl.BlockSpec((depth, D_MODEL, IN_PACKED), lambda i: (0, 0, 0)),
        compiler_params=pltpu.CompilerParams(
            dimension_semantics=("arbitrary",), vmem_limit_bytes=VMEM_LIMIT_BYTES),
        name="repack_w_in",
    )(w_cols)


def _pack_weights(p):
    depth = p["w_in"].shape[0]
    pad_rows = lambda a, n: jnp.concatenate(
        [a, jnp.zeros((depth, n - a.shape[1], a.shape[2]), F32)], axis=1)
    rep = lambda a: jnp.repeat(a, HEAD_DIM, axis=1)
    vec = jnp.stack([rep(p["ssd_dt_bias"]), rep(p["ssd_a_log"]), rep(p["ssd_d"]),
                     p["ssd_norm"], p["ret_norm"], p["gla_norm"],
                     p["lru_b_a"], p["lru_b_x"], p["lru_lambda"]], axis=1)
    eye = jnp.eye(N_HEADS, dtype=F32)
    bd = lambda a: jnp.einsum("lhij,hg->lhigj", a, eye).reshape(depth, GROUP_WIDTH, GROUP_WIDTH).astype(BF16)
    return dict(
        norm_mix=p["norm_mix"][:, None, :], w_in=_repack_w_in(p["w_in"]),
        cw_ssd=pad_rows(jnp.concatenate([p["ssd_conv_w"], p["ssd_conv_b"][:, None, :]], axis=1), 8),
        cw_lru=pad_rows(jnp.concatenate([p["lru_conv_w"], p["lru_conv_b"][:, None, :]], axis=1), 8),
        vec=pad_rows(vec, 16), bg=pad_rows(p["gla_b_gate"][:, None, :], 8),
        w2=pad_rows(p["gla_w_gate2"], 128).astype(BF16),
        wa=bd(p["lru_w_a"]), wx=bd(p["lru_w_x"]), w_out=p["w_out"].astype(BF16))


def _rope_tables(pos):
    half = HEAD_DIM // 2
    inv = ROPE_BASE ** (-jnp.arange(half, dtype=F32) / half)
    ang = pos.astype(F32)[:, None] * inv[None, :]
    cos, sin = jnp.cos(ang), jnp.sin(ang)
    cos = jnp.tile(jnp.concatenate([cos, cos], axis=1), (1, N_HEADS))
    sin = jnp.tile(jnp.concatenate([-sin, sin], axis=1), (1, N_HEADS))
    return cos, sin


def _prompt_mixer(h, cos, sin, layer, wts, consts, tq):
    b, l, _ = h.shape
    weights = [wts[n] for n in WEIGHT_NAMES]
    cvals = [consts[n] for n in CONST_NAMES]
    row = lambda w: pl.BlockSpec((None, tq, w), lambda bi, ti: (bi, ti, 0))
    per_seq = lambda r, w: pl.BlockSpec((None, r, w), lambda bi, ti: (bi, 0, 0))
    out_shapes = (jax.ShapeDtypeStruct((b, l, D_MODEL), F32),
                  jax.ShapeDtypeStruct((b, 8, SSD_CONV_DIM), F32),
                  jax.ShapeDtypeStruct((b, 8, GROUP_WIDTH), F32),
                  jax.ShapeDtypeStruct((b, 8, GROUP_WIDTH), F32),
                  jax.ShapeDtypeStruct((b, 256, 256), F32),
                  jax.ShapeDtypeStruct((b, 256, 256), F32),
                  jax.ShapeDtypeStruct((b, 128, 256), F32))
    return pl.pallas_call(
        functools.partial(_mixer_kernel, n_chunks=tq // CHUNK, level_sizes=_level_sizes(CHUNK)),
        out_shape=out_shapes,
        grid=(b, l // tq),
        in_specs=[row(D_MODEL),
                  pl.BlockSpec((tq, GROUP_WIDTH), lambda bi, ti: (ti, 0)),
                  pl.BlockSpec((tq, GROUP_WIDTH), lambda bi, ti: (ti, 0))]
                 + [_layer_spec(a, layer, 2) for a in weights] + [_full_spec(a, 2) for a in cvals],
        out_specs=(row(D_MODEL), per_seq(8, SSD_CONV_DIM), per_seq(8, GROUP_WIDTH),
                   per_seq(8, GROUP_WIDTH), per_seq(256, 256), per_seq(256, 256), per_seq(128, 256)),
        scratch_shapes=[pltpu.VMEM((8 + tq, SSD_CONV_DIM), F32), pltpu.VMEM((8 + tq, GROUP_WIDTH), F32)],
        compiler_params=pltpu.CompilerParams(
            dimension_semantics=("arbitrary", "arbitrary"), vmem_limit_bytes=VMEM_LIMIT_BYTES),
        name="prompt_mixer",
    )(h, cos, sin, *weights, *cvals)


def _sample_pre(hs, cos, sin, cst_ssd, cst_lru, h0, layer, wts):
    t, b = SAMPLE_T, SAMPLE_B
    weights = [wts[n] for n in WEIGHT_NAMES]
    fm = lambda f: jax.ShapeDtypeStruct((t, f, b), F32)
    out_shapes = (fm(256),) * 7 + (fm(128), fm(128), fm(256), fm(128)) + (
        jax.ShapeDtypeStruct((t * b, 1280), F32),
        jax.ShapeDtypeStruct((3, b, SSD_CONV_DIM), F32), jax.ShapeDtypeStruct((3, b, GROUP_WIDTH), F32),
        jax.ShapeDtypeStruct((b, GROUP_WIDTH), F32))
    lay = lambda a: pl.BlockSpec((None,) + a.shape[1:], lambda i: (layer,) + (0,) * (a.ndim - 1))
    return pl.pallas_call(
        _sample_pre_kernel,
        out_shape=out_shapes,
        grid=(1,),
        in_specs=[_full_spec(hs, 1), _full_spec(cos, 1), _full_spec(sin, 1), lay(cst_ssd), lay(cst_lru), lay(h0)]
                 + [_layer_spec(a, layer, 1) for a in weights],
        out_specs=tuple(pl.BlockSpec(o.shape, lambda i, n=len(o.shape): (0,) * n) for o in out_shapes),
        compiler_params=pltpu.CompilerParams(
            dimension_semantics=("arbitrary",), vmem_limit_bytes=VMEM_LIMIT_BYTES),
        name="sample_pre",
    )(hs, cos, sin, cst_ssd, cst_lru, h0, *weights)


def _sample_state(ops, gam, states, new_states, layer):
    t, b = SAMPLE_T, SAMPLE_B
    ssd_q, ssd_k, ssd_v, ssd_a, ret_q, ret_k, ret_v, gla_q, gla_k, gla_v, gla_g = ops
    first = new_states is None
    assert first == (layer == 0)
    aliased = [] if first else list(new_states)
    blk = lambda f: pl.BlockSpec((t, f // N_HEADS, b), lambda h: (0, h, 0))
    st = lambda dk: pl.BlockSpec((None, None, dk, HEAD_DIM, b), lambda h: (layer, h, 0, 0, 0))
    operands = [ssd_q, ssd_k, ssd_v, ssd_a, ret_q, ret_k, ret_v, gam, gla_q, gla_k, gla_v, gla_g, *states]
    in_specs = ([blk(256)] * 7 + [pl.BlockSpec((None, 8, b), lambda h: (h, 0, 0))]
                + [blk(128), blk(128), blk(256), blk(128), st(64), st(64), st(32)]
                + [pl.BlockSpec(memory_space=pl.ANY) for _ in aliased])
    out_shapes = (jax.ShapeDtypeStruct((t, 256, b), F32),) * 3 + tuple(
        jax.ShapeDtypeStruct(s.shape, F32) for s in states)
    depth = states[0].shape[0]
    st_all = lambda dk: pl.BlockSpec((depth, None, dk, HEAD_DIM, b), lambda h: (0, h, 0, 0, 0))
    st_out = st_all if first else st
    return pl.pallas_call(
        functools.partial(_sample_state_kernel, first_layer=first),
        out_shape=out_shapes,
        grid=(N_HEADS,),
        in_specs=in_specs,
        out_specs=(blk(256), blk(256), blk(256), st_out(64), st_out(64), st_out(32)),
        input_output_aliases={len(operands) + k: 3 + k for k in range(len(aliased))},
        compiler_params=pltpu.CompilerParams(
            dimension_semantics=("arbitrary",), vmem_limit_bytes=VMEM_LIMIT_BYTES),
        name="sample_state",
    )(*operands, *aliased)


def _sample_post(hs, side, o_ssd, o_ret, o_gla, layer, wts, mavg, fw, norm_final, final_norm):
    lay = [wts["vec"], wts["w_out"]]
    ffn_w = [fw[n] for n in ("norm_ffn", "wg", "wu", "wd")]
    acts = [hs, side, o_ssd, o_ret, o_gla]
    return pl.pallas_call(
        functools.partial(_sample_post_kernel, final_norm=final_norm),
        out_shape=jax.ShapeDtypeStruct(hs.shape, F32),
        grid=(1,),
        in_specs=[_full_spec(a, 1) for a in acts] + [_layer_spec(a, layer, 1) for a in lay]
                 + [_full_spec(mavg, 1)] + [_layer_spec(a, layer, 1) for a in ffn_w]
                 + [_full_spec(norm_final, 1)],
        out_specs=_full_spec(hs, 1),
        compiler_params=pltpu.CompilerParams(
            dimension_semantics=("arbitrary",), vmem_limit_bytes=VMEM_LIMIT_BYTES),
        name="sample_post",
    )(*acts, *lay, mavg, *ffn_w, norm_final)


def _diag_blocks(s, dk):
    return jnp.stack([s[:, h * dk:(h + 1) * dk, h * HEAD_DIM:(h + 1) * HEAD_DIM]
                      for h in range(N_HEADS)], axis=1)


def kernel(x_prompt, x_sample, state_ssd_conv, state_ssd, state_ret, state_gla, state_lru_conv, state_lru,
           norm_mix, w_in, ssd_conv_w, ssd_conv_b, ssd_dt_bias, ssd_a_log, ssd_d, ssd_norm, ret_norm,
           gla_w_gate2, gla_b_gate, gla_norm, lru_conv_w, lru_conv_b, lru_w_a, lru_b_a, lru_w_x, lru_b_x,
           lru_lambda, w_out, norm_ffn, w_gate, w_up, w_down, norm_final):
    params = dict(norm_mix=norm_mix, w_in=w_in, ssd_conv_w=ssd_conv_w, ssd_conv_b=ssd_conv_b,
                  ssd_dt_bias=ssd_dt_bias, ssd_a_log=ssd_a_log, ssd_d=ssd_d, ssd_norm=ssd_norm,
                  ret_norm=ret_norm, gla_w_gate2=gla_w_gate2, gla_b_gate=gla_b_gate, gla_norm=gla_norm,
                  lru_conv_w=lru_conv_w, lru_conv_b=lru_conv_b, lru_w_a=lru_w_a, lru_b_a=lru_b_a,
                  lru_w_x=lru_w_x, lru_b_x=lru_b_x, lru_lambda=lru_lambda, w_out=w_out)
    depth = w_in.shape[0]
    bp, lp, _ = x_prompt.shape
    bs, ls, _ = x_sample.shape
    assert (bs, ls) == (SAMPLE_B, SAMPLE_T) and lp % CHUNK == 0
    tq = 512 if lp % 512 == 0 else CHUNK
    tm_p = 512 if (bp * lp) % 512 == 0 else CHUNK

    wts = _pack_weights(params)
    fw = dict(norm_ffn=norm_ffn[:, None, :], wg=w_gate.astype(BF16), wu=w_up.astype(BF16),
              wd=w_down.astype(BF16))
    consts = _const_arrays()
    cos_p, sin_p = _rope_tables(jnp.arange(lp, dtype=jnp.int32))
    cos_s, sin_s = _rope_tables(PAST_LEN + jnp.arange(ls, dtype=jnp.int32))
    cos_s, sin_s = jnp.repeat(cos_s, bs, axis=0), jnp.repeat(sin_s, bs, axis=0)
    nfin = norm_final[None]
    gam = jnp.broadcast_to(jnp.exp(_ret_log_gamma())[:, None, None], (N_HEADS, 8, bs))
    cst_ssd = jnp.transpose(state_ssd_conv, (0, 2, 1, 3))
    cst_lru = jnp.transpose(state_lru_conv, (0, 2, 1, 3))
    states_s = tuple(jnp.transpose(a, (0, 2, 3, 4, 1)) for a in (state_ssd, state_ret, state_gla))

    hp = x_prompt
    hs = jnp.transpose(x_sample, (1, 0, 2)).reshape(ls * bs, D_MODEL)
    new_p = [[] for _ in range(6)]
    new_s = [[] for _ in range(3)]
    new_states_s = None
    for i in range(depth):
        last = i == depth - 1

        hmid, tail_ssd, tail_lru, hl, s_ssd, s_ret, s_gla = _prompt_mixer(
            hp, cos_p, sin_p, i, wts, consts, tq)
        hp = _ffn(hmid.reshape(bp * lp, D_MODEL), i, fw, nfin, last, tm_p).reshape(bp, lp, D_MODEL)
        for j, a in enumerate((tail_ssd[:, 5:], _diag_blocks(s_ssd, 64), _diag_blocks(s_ret, 64),
                               _diag_blocks(s_gla, 32), tail_lru[:, 5:], hl[:, 7])):
            new_p[j].append(a)

        *ops, side, nconv_ssd, nconv_lru, nh = _sample_pre(
            hs, cos_s, sin_s, cst_ssd, cst_lru, state_lru, i, wts)
        o_ssd, o_ret, o_gla, *new_states_s = _sample_state(ops, gam, states_s, new_states_s, i)
        hs = _sample_post(hs, side, o_ssd, o_ret, o_gla, i, wts, consts["mavg"], fw, nfin, last)
        for j, a in enumerate((nconv_ssd, nconv_lru, nh)):
            new_s[j].append(a)

    outs_p = [jnp.stack(a, axis=0) for a in new_p]
    s_conv, s_lru_conv, s_lru = [jnp.stack(a, axis=0) for a in new_s]
    n_ssd, n_ret, n_gla = (jnp.transpose(a, (0, 4, 1, 2, 3)) for a in new_states_s)
    y_sample = jnp.transpose(hs.reshape(ls, bs, D_MODEL), (1, 0, 2))
    return (hp, y_sample, *outs_p,
            jnp.transpose(s_conv, (0, 2, 1, 3)), n_ssd, n_ret, n_gla,
            jnp.transpose(s_lru_conv, (0, 2, 1, 3)), s_lru)
```

```python
import functools
import math

import numpy as np
import jax
import jax.numpy as jnp
from jax import lax
from jax.experimental import pallas as pl
from jax.experimental.pallas import tpu as pltpu

F32 = jnp.float32
BF16 = jnp.bfloat16

D_MODEL = 1024
HEAD_DIM = 64
N_HEADS = 4
GROUP_WIDTH = 256
EPS = 1e-6
SSD_CONV_DIM = 512
GLA_KEY_WIDTH = 128
GLA_DK = 32
GLA_GATE_RANK = 16
GLA_GATE_TEMP = 16.0
LRU_C = 8.0
ROPE_BASE = 10000.0
PAST_LEN = 16384
IN_DIM = 3092

C_Z, C_XBC, C_DT, C_RET, C_GQ, C_GK, C_GV, C_GR, C_LY, C_LX, C_MISC, IN_PACKED = (
    0, 256, 768, 1024, 2048, 2176, 2304, 2560, 2816, 3072, 3328, 3456)
S_DT, S_RET, S_GLA, S_GLR, S_GR, S_LRU = 768, 772, 1796, 2308, 2324, 2580

CHUNK = 128
GLA_CHUNK = 64
PROJ_ROWS, PROJ_COLS = 128, 256
FFN_COLS = 256
SAMPLE_B = 128
SAMPLE_T = 4
VMEM_LIMIT_BYTES = 56 * 1024 * 1024


def _level_sizes(seg):
    sizes = []
    n = seg
    while n >= 2:
        sizes.append(n)
        n //= 2
    return tuple(sizes)


@functools.lru_cache(maxsize=None)
def _chunk_constants():
    c, g = CHUNK, GLA_CHUNK
    i = np.arange(c)[:, None]
    j = np.arange(c)[None, :]
    gi = np.arange(g)[:, None]
    gj = np.arange(g)[None, :]
    up_levels, m_levels = [], []
    for n in _level_sizes(g):
        up_levels.append(np.broadcast_to((gi % n) >= n // 2, (g, GLA_KEY_WIDTH)))
        m_levels.append(np.tile((gi // n) == (gj // n), (1, N_HEADS)))
    m_levels.append(np.tile(gi == gj, (1, N_HEADS)))
    hh = np.arange(GROUP_WIDTH) // HEAD_DIM
    kh = np.arange(GLA_KEY_WIDTH) // GLA_DK
    stack_rows = lambda n: (np.arange(N_HEADS * n) // n)[:, None]
    return dict(
        tri=(j <= i).astype(np.float32),
        trig=((j <= i) & (i // g == j // g)).astype(np.float32),
        gmask=np.stack(m_levels).astype(np.float32),
        gup=np.stack(up_levels).astype(np.float32),
        gsgn=np.where(np.stack(up_levels), 1.0, -1.0).astype(np.float32),
        sm256=(stack_rows(c) == hh[None, :]).astype(np.float32),
        gsmk=(stack_rows(g) == kh[None, :]).astype(np.float32),
        gsmv=(stack_rows(g) == hh[None, :]).astype(np.float32),
        bdm=(hh[:, None] == hh[None, :]).astype(np.float32),
        bdg=(kh[:, None] == hh[None, :]).astype(np.float32))


def _ret_log_gamma():
    return np.log(1.0 - 2.0 ** (-5.0 - np.arange(N_HEADS, dtype=np.float64)))


def _const_arrays():
    k = _chunk_constants()
    c = CHUNK
    log_gamma = _ret_log_gamma()
    lg256 = np.repeat(log_gamma, HEAD_DIM)[None, :]
    pos = np.arange(c, dtype=np.float64)[:, None]
    dpos = pos - pos.T
    tri = k["tri"]
    dret = np.concatenate(
        [np.where(tri > 0, np.exp(dpos * log_gamma[h]), 0.0) for h in range(N_HEADS)], axis=1)
    arr = lambda a, dt=F32: jnp.asarray(np.asarray(a, np.float32), dt)
    return dict(
        tri=arr(tri, BF16), trig=arr(k["trig"], BF16), cmask=arr(tri), gmask=arr(k["gmask"], BF16),
        gupb=arr(k["gup"], BF16),
        gsgn=arr(k["gsgn"]),
        retq=arr(np.exp((pos + 1.0) * lg256)), retk=arr(np.exp((c - 1.0 - pos) * lg256)),
        rets=arr(np.broadcast_to(np.exp(c * lg256), (8, GROUP_WIDTH))), dret=arr(dret),
        sm256=arr(k["sm256"], BF16), gsmk=arr(k["gsmk"], BF16), gsmv=arr(k["gsmv"], BF16),
        bdm=arr(k["bdm"]), bdg=arr(k["bdg"]), mavg=arr(k["bdm"] / HEAD_DIM, BF16))


CONST_NAMES = ("tri", "trig", "cmask", "gmask", "gupb", "gsgn", "retq", "retk", "rets", "dret",
               "sm256", "gsmk", "gsmv", "bdm", "bdg", "mavg")
WEIGHT_NAMES = ("norm_mix", "w_in", "cw_ssd", "cw_lru", "vec", "bg", "w2", "wa", "wx", "w_out")

V_DTB, V_ALOG, V_SSDD, V_SSDN, V_RETN, V_GLAN, V_LBA, V_LBX, V_LLAM = range(9)


def _dot(a, b):
    return jnp.dot(a, b, preferred_element_type=F32)


def _dot_nt(a, b):
    return lax.dot_general(a, b, (((1,), (1,)), ((), ())), preferred_element_type=F32)


def _dot_tn(a, b):
    return lax.dot_general(a, b, (((0,), (0,)), ((), ())), preferred_element_type=F32)


def _split3(x):
    hi = x.astype(BF16)
    r1 = x - hi.astype(F32)
    mid = r1.astype(BF16)
    lo = (r1 - mid.astype(F32)).astype(BF16)
    return hi, mid, lo


def _dot3(w01, x):
    hi, mid, lo = _split3(x)
    return _dot(w01, hi) + _dot(w01, mid) + _dot(w01, lo)


def _sigmoid(x):
    return 1.0 / (1.0 + jnp.exp(-x))


def _silu(x):
    return x * _sigmoid(x)


def _log1pexp_neg_abs(x):
    return jnp.log(1.0 + jnp.exp(-jnp.abs(x)))


def _softplus(x):
    return jnp.maximum(x, 0.0) + _log1pexp_neg_abs(x)


def _gelu_tanh(x):
    return 0.5 * x * (1.0 + jnp.tanh(math.sqrt(2.0 / math.pi) * (x + 0.044715 * (x * x * x))))


def _rms(x, w):
    return x * lax.rsqrt(jnp.mean(x * x, axis=-1, keepdims=True) + EPS) * w


def _stack4(x):
    return jnp.concatenate([x, x, x, x], axis=0)


def _causal_conv_carried(x, w_ref, buf_ref):
    rows = x.shape[0]
    buf_ref[8:8 + rows, :] = x
    y = w_ref[4:5, :] + x * w_ref[3:4, :]
    for k in (1, 2, 3):
        y = y + buf_ref[8 - k:8 - k + rows, :] * w_ref[3 - k:4 - k, :]
    buf_ref[0:8, :] = x[rows - 8:, :]
    return y


def _swap_halves(x):
    w = x.shape[1]
    lane = lax.broadcasted_iota(jnp.int32, x.shape, 1)
    return jnp.where((lane & 32) == 0, pltpu.roll(x, w - 32, axis=1), pltpu.roll(x, 32, axis=1))


def _expand_groups(x):
    lane = lax.broadcasted_iota(jnp.int32, x.shape, 1)
    r = pltpu.roll(x, 64, axis=1)
    first = lane < 64
    return jnp.concatenate([jnp.where(first, x, r), jnp.where(first, r, x)], axis=1)


def _head_cols(cum):
    cols = []
    for half in range(2):
        x = cum[:, half * 128:(half + 1) * 128]
        lane = lax.broadcasted_iota(jnp.int32, x.shape, 1)
        r = pltpu.roll(x, 64, axis=1)
        first = lane < 64
        cols.append(jnp.where(first, x, r))
        cols.append(jnp.where(first, r, x))
    return cols


def _mid_cum(cum, n):
    rows, w = cum.shape
    half = n // 2
    if n >= 16:
        blocks = cum.reshape(rows // n, n, w)
        return jnp.broadcast_to(blocks[:, half - 1:half, :], blocks.shape).reshape(rows, w)
    pos = lax.broadcasted_iota(jnp.int32, cum.shape, 0) & (n - 1)
    out = cum
    for p in range(n):
        delta = half - 1 - p
        if delta != 0:
            out = jnp.where(pos == p, pltpu.roll(cum, (-delta) % rows, axis=0), out)
    return out


def _scan_rows(a, u, h_in):
    rows, w = a.shape
    pos = lax.broadcasted_iota(jnp.int32, a.shape, 0) & 7
    for s in (1, 2, 4):
        keep = pos >= s
        a_s = jnp.where(keep, pltpu.roll(a, s, axis=0), 1.0)
        u_s = jnp.where(keep, pltpu.roll(u, s, axis=0), 0.0)
        u = a * u_s + u
        a = a * a_s
    groups = rows // 8
    a3, u3 = a.reshape(groups, 8, w), u.reshape(groups, 8, w)
    carries = []
    h = h_in
    for g in range(groups):
        carries.append(h)
        h = a3[g, 7:8, :] * h + u3[g, 7:8, :]
    carry = jnp.broadcast_to(jnp.stack(carries, axis=0), (groups, 8, w))
    return (u3 + a3 * carry).reshape(rows, w)


def _ffn_math(x, nf_ref, wg_ref, wu_ref, wd_ref, nfin_ref, final_norm):
    hn = _rms(x, nf_ref[...]).astype(BF16)
    d_ff = wg_ref.shape[1]
    acts = []
    for c0 in range(0, d_ff, FFN_COLS):
        c1 = min(c0 + FFN_COLS, d_ff)
        g = _dot(hn, wg_ref[:, c0:c1])
        up = _dot(hn, wu_ref[:, c0:c1])
        acts.append((_silu(g) * up).astype(BF16))
    y = x + _dot(jnp.concatenate(acts, axis=1), wd_ref[...])
    return _rms(y, nfin_ref[...]) if final_norm else y


def _ffn_kernel(x_ref, nf_ref, wg_ref, wu_ref, wd_ref, nfin_ref, o_ref, *, final_norm):
    o_ref[...] = _ffn_math(x_ref[...], nf_ref, wg_ref, wu_ref, wd_ref, nfin_ref, final_norm)


def _prep(cols, conv_ssd_fn, conv_lru_fn, cos, sin, W):
    vec = W["vec"]
    xbc_raw, lru_x = cols(C_XBC, 512), cols(C_LX, 256)
    xbc = _silu(conv_ssd_fn(xbc_raw))
    xs = xbc[:, :256]
    dt = _softplus(cols(C_DT, 256) + vec[V_DTB:V_DTB + 1, :])
    ret_q, ret_k = cols(C_RET, 256), cols(C_RET + 256, 256)
    gate_x = _dot(cols(C_MISC, 128).astype(BF16), W["w2"][...]) + W["bg"][0:1, :]
    xc = conv_lru_fn(lru_x)
    xcb = xc.astype(BF16)
    lr = _sigmoid(_dot(xcb, W["wa"][...]) + vec[V_LBA:V_LBA + 1, :])
    li = _sigmoid(_dot(xcb, W["wx"][...]) + vec[V_LBX:V_LBX + 1, :])
    l_loga = -LRU_C * lr * _softplus(-vec[V_LLAM:V_LLAM + 1, :])
    l_a = jnp.exp(l_loga)
    return dict(
        xbc_raw=xbc_raw, lru_x=lru_x, xs=xs,
        ssd_k=_expand_groups(xbc[:, 256:384]), ssd_q=_expand_groups(xbc[:, 384:512]),
        ssd_la=-dt * jnp.exp(vec[V_ALOG:V_ALOG + 1, :]),
        ssd_v=xs * dt,
        rq=ret_q * cos + _swap_halves(ret_q) * sin,
        rk=(ret_k * cos + _swap_halves(ret_k) * sin) * (HEAD_DIM ** -0.5),
        gla_lg=(jnp.minimum(gate_x, 0.0) - _log1pexp_neg_abs(gate_x)) / GLA_GATE_TEMP,
        gq=cols(C_GQ, 128) * (GLA_DK ** -0.5),
        l_a=l_a,
        l_u=jnp.sqrt(-jnp.tanh(l_loga) * (l_a * l_a + 1.0)) * (li * xc))


def _mix_out(x, o_ssd, o_ret, o_gla, y_lru, xs, g_ssd, g_ret, g_gla, W, mavg):
    vec = W["vec"]
    y_ssd = _rms((o_ssd + xs * vec[V_SSDD:V_SSDD + 1, :]) * g_ssd, vec[V_SSDN:V_SSDN + 1, :])
    mu = _dot(o_ret.astype(BF16), mavg)
    d = o_ret - mu
    var = _dot((d * d).astype(BF16), mavg)
    y_ret = d * lax.rsqrt(var + EPS) * vec[V_RETN:V_RETN + 1, :] * g_ret
    ms = _dot((o_gla * o_gla).astype(BF16), mavg)
    y_gla = o_gla * lax.rsqrt(ms + EPS) * vec[V_GLAN:V_GLAN + 1, :] * g_gla
    ymix = jnp.concatenate([y_ssd, y_ret, y_gla, y_lru], axis=1).astype(BF16)
    return x + _dot(ymix, W["w_out"][...])


def _mixer_kernel(*refs, n_chunks, level_sizes):
    it = iter(refs)
    x_ref, cos_ref, sin_ref = next(it), next(it), next(it)
    W = {n: next(it) for n in WEIGHT_NAMES}
    K = {n: next(it) for n in CONST_NAMES}
    hmid_ref = next(it)
    tail_ssd_ref, tail_lru_ref, hl_ref = next(it), next(it), next(it)
    S_ssd_ref, S_ret_ref, S_gla_ref = next(it), next(it), next(it)
    cbuf_ssd_ref, cbuf_lru_ref = next(it), next(it)

    C = CHUNK
    n_levels = len(level_sizes)

    @pl.when(pl.program_id(1) == 0)
    def _():
        cbuf_ssd_ref[0:8, :] = jnp.zeros((8, SSD_CONV_DIM), F32)
        cbuf_lru_ref[0:8, :] = jnp.zeros((8, GROUP_WIDTH), F32)
        hl_ref[...] = jnp.zeros_like(hl_ref)
        S_ssd_ref[...] = jnp.zeros_like(S_ssd_ref)
        S_ret_ref[...] = jnp.zeros_like(S_ret_ref)
        S_gla_ref[...] = jnp.zeros_like(S_gla_ref)

    x = x_ref[...]
    T = x.shape[0]
    xn = _rms(x, W["norm_mix"][...]).astype(BF16)
    u_blocks = {}

    def cols(a, w):
        parts = []
        for c0 in range(a, a + w, 128):
            if c0 not in u_blocks:
                base = (c0 // PROJ_COLS) * PROJ_COLS
                width = min(PROJ_COLS, IN_PACKED - base)
                wc = W["w_in"][:, base:base + width]
                full = jnp.concatenate(
                    [_dot(xn[r:r + PROJ_ROWS, :], wc) for r in range(0, T, PROJ_ROWS)], axis=0)
                for k in range(width // 128):
                    u_blocks[base + 128 * k] = full[:, 128 * k:128 * (k + 1)]
            parts.append(u_blocks[c0])
        return parts[0] if len(parts) == 1 else jnp.concatenate(parts, axis=1)

    pr = _prep(cols, lambda v: _causal_conv_carried(v, W["cw_ssd"], cbuf_ssd_ref),
               lambda v: _causal_conv_carried(v, W["cw_lru"], cbuf_lru_ref),
               cos_ref[...], sin_ref[...], W)
    ret_v, gla_k, gla_v = cols(C_RET + 512, 256), cols(C_GK, 128), cols(C_GV, 256)
    l_h = _scan_rows(pr["l_a"], pr["l_u"], hl_ref[7:8, :])
    y_lru = l_h * _gelu_tanh(cols(C_LY, 256))

    cmask = K["cmask"][...] > 0.0
    sm256 = K["sm256"][...]
    gsmk = K["gsmk"][...]
    gsmv = K["gsmv"][...]

    def stack_kv(a, mask):
        return _stack4(a.astype(BF16)) * mask

    def cum_and_rest(la):
        cum = _dot3(K["tri"][...], la)
        return cum, cum[C - 1:C, :] - cum

    o_ssd, o_ret, o_gla = [], [], []
    for c in range(n_chunks):
        sl = slice(c * C, (c + 1) * C)

        cum, rem = cum_and_rest(pr["ssd_la"][sl])
        q, k, v = pr["ssd_q"][sl], pr["ssd_k"][sl], pr["ssd_v"][sl]
        dec = jnp.concatenate(
            [jnp.exp(jnp.where(cmask, col - col.T, -jnp.inf)) for col in _head_cols(cum)], axis=1)
        p = _dot_nt(q.astype(BF16), stack_kv(k, sm256)) * dec
        a_cum = jnp.exp(cum)
        s_old = S_ssd_ref[...]
        o = _dot(p.astype(BF16), stack_kv(v, sm256)) + _dot((q * a_cum).astype(BF16), s_old.astype(BF16))
        ds = _dot_tn((k * jnp.exp(rem)).astype(BF16), v.astype(BF16)) * K["bdm"][...]
        S_ssd_ref[...] = s_old * a_cum[C - 1:C, :] + ds
        o_ssd.append(o)

        q, k, v = pr["rq"][sl], pr["rk"][sl], ret_v[sl]
        p = _dot_nt(q.astype(BF16), stack_kv(k, sm256)) * K["dret"][...]
        s_old = S_ret_ref[...]
        o = (_dot(p.astype(BF16), stack_kv(v, sm256))
             + _dot((q * K["retq"][...]).astype(BF16), s_old.astype(BF16)))
        ds = _dot_tn((k * K["retk"][...]).astype(BF16), v.astype(BF16)) * K["bdm"][...]
        S_ret_ref[...] = s_old * K["rets"][0:1, :] + ds
        o_ret.append(o)

        G = GLA_CHUNK
        cum_all = _dot3(K["trig"][...], pr["gla_lg"][sl])
        a_all = jnp.exp(cum_all)
        a_all_t = a_all.T
        outs = []
        for g0 in range(0, C, G):
            gs = slice(c * C + g0, c * C + g0 + G)
            cum, a_cum = cum_all[g0:g0 + G], a_all[g0:g0 + G]
            q, k, v = pr["gq"][gs], gla_k[gs], gla_v[gs]
            qb, kb = q.astype(BF16), k.astype(BF16)
            p = _dot_nt(qb, _stack4(kb) * gsmk).astype(BF16) * K["gmask"][n_levels]
            for lvl, n in enumerate(level_sizes):
                mid = _mid_cum(cum, n)
                up = K["gupb"][lvl] > 0.0
                f = jnp.exp((cum - mid) * K["gsgn"][lvl]).astype(BF16)
                zero = jnp.zeros_like(qb)
                ql = jnp.where(up, qb * f, zero)
                kl = jnp.where(up, zero, kb * f)
                sc = _dot_nt(ql, _stack4(kl) * gsmk).astype(BF16)
                p = p + (sc if n == G else sc * K["gmask"][lvl])
            s_old = S_gla_ref[...]
            outs.append(_dot(p, _stack4(v.astype(BF16)) * gsmv)
                        + _dot((q * a_cum).astype(BF16), s_old.astype(BF16)))
            rem = cum[G - 1:G, :] - cum
            ds = _dot_tn((k * jnp.exp(rem)).astype(BF16), v.astype(BF16)) * K["bdg"][...]
            S_gla_ref[...] = s_old * a_all_t[:, g0 + G - 1:g0 + G] + ds
        o = jnp.concatenate(outs, axis=0)
        o_gla.append(o)

    cat = lambda parts: parts[0] if n_chunks == 1 else jnp.concatenate(parts, axis=0)
    hmid_ref[...] = _mix_out(x, cat(o_ssd), cat(o_ret), cat(o_gla), y_lru, pr["xs"],
                             _silu(cols(C_Z, 256)), _silu(cols(C_RET + 768, 256)), _silu(cols(C_GR, 256)),
                             W, K["mavg"][...])

    tail_ssd_ref[...] = pr["xbc_raw"][T - 8:, :]
    tail_lru_ref[...] = pr["lru_x"][T - 8:, :]
    hl_ref[...] = l_h[T - 8:, :]


def _feature_major(a, out_ref):
    nb = SAMPLE_B
    for t in range(SAMPLE_T):
        for fb in range(a.shape[1] // 128):
            out_ref[t, fb * 128:(fb + 1) * 128, :] = a[t * nb:(t + 1) * nb, fb * 128:(fb + 1) * 128].T


def _token_major(o_ref):
    f = o_ref.shape[1]
    return jnp.concatenate(
        [jnp.concatenate([o_ref[t, fb * 128:(fb + 1) * 128, :].T for fb in range(f // 128)], axis=1)
         for t in range(SAMPLE_T)], axis=0)


def _sample_pre_kernel(*refs):
    it = iter(refs)
    x_ref, cos_ref, sin_ref, cst_ssd_ref, cst_lru_ref, h0_ref = (next(it) for _ in range(6))
    W = {n: next(it) for n in WEIGHT_NAMES}
    (ssd_q_ref, ssd_k_ref, ssd_v_ref, ssd_a_ref, ret_q_ref, ret_k_ref, ret_v_ref,
     gla_q_ref, gla_k_ref, gla_v_ref, gla_g_ref, side_ref, nconv_ssd_ref, nconv_lru_ref, nh_ref) = (
         next(it) for _ in range(15))
    nb = SAMPLE_B

    u = _dot(_rms(x_ref[...], W["norm_mix"][...]).astype(BF16), W["w_in"][...])
    cols = lambda a, w: u[:, a:a + w]

    def conv(x, w_ref, cst_ref):
        ext = jnp.concatenate([cst_ref[0], cst_ref[1], cst_ref[2], x], axis=0)
        y = w_ref[4:5, :] + x * w_ref[3:4, :]
        for j in range(3):
            y = y + ext[j * nb:(j + SAMPLE_T) * nb, :] * w_ref[j:j + 1, :]
        return y

    pr = _prep(cols, lambda v: conv(v, W["cw_ssd"], cst_ssd_ref), lambda v: conv(v, W["cw_lru"], cst_lru_ref),
               cos_ref[...], sin_ref[...], W)
    nconv_ssd_ref[...] = pr["xbc_raw"][nb:, :].reshape(3, nb, SSD_CONV_DIM)
    nconv_lru_ref[...] = pr["lru_x"][nb:, :].reshape(3, nb, GROUP_WIDTH)

    h = h0_ref[...]
    hs = []
    for t in range(SAMPLE_T):
        h = pr["l_a"][t * nb:(t + 1) * nb, :] * h + pr["l_u"][t * nb:(t + 1) * nb, :]
        hs.append(h)
    nh_ref[...] = h
    y_lru = jnp.concatenate(hs, axis=0) * _gelu_tanh(cols(C_LY, 256))

    _feature_major(pr["ssd_q"], ssd_q_ref)
    _feature_major(pr["ssd_k"], ssd_k_ref)
    _feature_major(pr["ssd_v"], ssd_v_ref)
    _feature_major(jnp.exp(pr["ssd_la"]), ssd_a_ref)
    _feature_major(pr["rq"], ret_q_ref)
    _feature_major(pr["rk"], ret_k_ref)
    _feature_major(cols(C_RET + 512, 256), ret_v_ref)
    _feature_major(pr["gq"], gla_q_ref)
    _feature_major(cols(C_GK, 128), gla_k_ref)
    _feature_major(cols(C_GV, 256), gla_v_ref)
    _feature_major(jnp.exp(pr["gla_lg"]), gla_g_ref)
    side_ref[:, 0:256] = pr["xs"]
    side_ref[:, 256:512] = _silu(cols(C_Z, 256))
    side_ref[:, 512:768] = _silu(cols(C_RET + 768, 256))
    side_ref[:, 768:1024] = _silu(cols(C_GR, 256))
    side_ref[:, 1024:1280] = y_lru


def _sample_state_kernel(ssd_q_ref, ssd_k_ref, ssd_v_ref, ssd_a_ref, ret_q_ref, ret_k_ref, ret_v_ref, gam_ref,
                         gla_q_ref, gla_k_ref, gla_v_ref, gla_g_ref, s_ssd_ref, s_ret_ref, s_gla_ref,
                         *rest, first_layer):
    o_ssd_ref, o_ret_ref, o_gla_ref, n_ssd_ref, n_ret_ref, n_gla_ref = rest[-6:]

    def run(q_ref, k_ref, v_ref, decay, s_ref, n_ref, o_ref):
        if first_layer:
            for l in range(1, n_ref.shape[0]):
                n_ref[l] = jnp.zeros(n_ref.shape[1:], F32)
            n_ref = n_ref.at[0]

        def body(d, acc):
            s = s_ref[d]
            new = []
            for t in range(SAMPLE_T):
                s = decay(t, d) * s + k_ref[t, pl.ds(d, 1), :] * v_ref[t]
                new.append(acc[t] + q_ref[t, pl.ds(d, 1), :] * s)
            n_ref[d] = s
            return tuple(new)

        zero = jnp.zeros((HEAD_DIM, SAMPLE_B), F32)
        acc = lax.fori_loop(0, s_ref.shape[0], body, (zero,) * SAMPLE_T, unroll=2)
        for t in range(SAMPLE_T):
            o_ref[t] = acc[t]

    run(ssd_q_ref, ssd_k_ref, ssd_v_ref, lambda t, d: ssd_a_ref[t, 0:1, :], s_ssd_ref, n_ssd_ref, o_ssd_ref)
    run(ret_q_ref, ret_k_ref, ret_v_ref, lambda t, d: gam_ref[0:1, :], s_ret_ref, n_ret_ref, o_ret_ref)
    run(gla_q_ref, gla_k_ref, gla_v_ref, lambda t, d: gla_g_ref[t, pl.ds(d, 1), :], s_gla_ref, n_gla_ref,
        o_gla_ref)


def _sample_post_kernel(x_ref, side_ref, o_ssd_ref, o_ret_ref, o_gla_ref, vec_ref, wout_ref, mavg_ref,
                        nf_ref, wg_ref, wu_ref, wd_ref, nfin_ref, out_ref, *, final_norm):
    side = lambda i: side_ref[:, i * 256:(i + 1) * 256]
    W = dict(vec=vec_ref, w_out=wout_ref)
    hmid = _mix_out(x_ref[...], _token_major(o_ssd_ref), _token_major(o_ret_ref), _token_major(o_gla_ref),
                    side(4), side(0), side(1), side(2), side(3), W, mavg_ref[...])
    out_ref[...] = _ffn_math(hmid, nf_ref, wg_ref, wu_ref, wd_ref, nfin_ref, final_norm)


def _layer_spec(a, layer, grid_rank):
    idx = (layer,) + (0,) * (a.ndim - 1)
    imap = (lambda i: idx) if grid_rank == 1 else (lambda b, t: idx)
    return pl.BlockSpec((None,) + a.shape[1:], imap, pipeline_mode=pl.Buffered(1))


def _full_spec(a, grid_rank):
    zeros = (0,) * a.ndim
    return pl.BlockSpec(a.shape, (lambda i: zeros) if grid_rank == 1 else (lambda b, t: zeros))


def _ffn(h2d, layer, fw, norm_final, final_norm, tm):
    rows = h2d.shape[0]
    weights = [fw[n] for n in ("norm_ffn", "wg", "wu", "wd")]
    return pl.pallas_call(
        functools.partial(_ffn_kernel, final_norm=final_norm),
        out_shape=jax.ShapeDtypeStruct(h2d.shape, F32),
        grid=(rows // tm,),
        in_specs=[pl.BlockSpec((tm, D_MODEL), lambda i: (i, 0))]
                 + [_layer_spec(a, layer, 1) for a in weights] + [_full_spec(norm_final, 1)],
        out_specs=pl.BlockSpec((tm, D_MODEL), lambda i: (i, 0)),
        compiler_params=pltpu.CompilerParams(
            dimension_semantics=("arbitrary",), vmem_limit_bytes=VMEM_LIMIT_BYTES),
        name="ffn",
    )(h2d, *weights, norm_final)


def _repack_kernel(w_ref, o_ref, *, depth):
    lane = lax.broadcasted_iota(jnp.int32, (D_MODEL, 128), 1)
    for l in range(depth):
        block = lambda src: w_ref[src:src + 128, l, :].T
        put = lambda dst, val: o_ref.__setitem__((l, slice(None), slice(dst, dst + 128)), val.astype(BF16))
        for dst, src, n in ((C_Z, 0, 768), (C_RET, S_RET, 1024), (C_GQ, S_GLA, 512),
                            (C_GR, S_GR, 256), (C_LY, S_LRU, 512)):
            for off in range(0, n, 128):
                put(dst + off, block(src + off))
        dtb = block(S_DT)
        for half in range(2):
            first = jnp.broadcast_to(dtb[:, 2 * half:2 * half + 1], (D_MODEL, 128))
            second = jnp.broadcast_to(dtb[:, 2 * half + 1:2 * half + 2], (D_MODEL, 128))
            put(C_DT + 128 * half, jnp.where(lane < HEAD_DIM, first, second))
        base = (S_GLR // 128) * 128
        win = pltpu.roll(block(base), 128 - (S_GLR - base), axis=1)
        put(C_MISC, jnp.where(lane < GLA_GATE_RANK, win, 0.0))


def _repack_w_in(w_in):
    depth = w_in.shape[0]
    w_cols = jnp.transpose(w_in, (2, 0, 1))
    return pl.pallas_call(
        functools.partial(_repack_kernel, depth=depth),
        out_shape=jax.ShapeDtypeStruct((depth, D_MODEL, IN_PACKED), BF16),
        grid=(1,),
        in_specs=[_full_spec(w_cols, 1)],
        out_specs=pl.BlockSpec((depth, D_MODEL, IN_PACKED), lambda i: (0, 0, 0)),
        compiler_params=pltpu.CompilerParams(
            dimension_semantics=("arbitrary",), vmem_limit_bytes=VMEM_LIMIT_BYTES),
        name="repack_w_in",
    )(w_cols)


def _pack_weights(p):
    depth = p["w_in"].shape[0]
    pad_rows = lambda a, n: jnp.concatenate(
        [a, jnp.zeros((depth, n - a.shape[1], a.shape[2]), F32)], axis=1)
    rep = lambda a: jnp.repeat(a, HEAD_DIM, axis=1)
    vec = jnp.stack([rep(p["ssd_dt_bias"]), rep(p["ssd_a_log"]), rep(p["ssd_d"]),
                     p["ssd_norm"], p["ret_norm"], p["gla_norm"],
                     p["lru_b_a"], p["lru_b_x"], p["lru_lambda"]], axis=1)
    eye = jnp.eye(N_HEADS, dtype=F32)
    bd = lambda a: jnp.einsum("lhij,hg->lhigj", a, eye).reshape(depth, GROUP_WIDTH, GROUP_WIDTH).astype(BF16)
    return dict(
        norm_mix=p["norm_mix"][:, None, :], w_in=_repack_w_in(p["w_in"]),
        cw_ssd=pad_rows(jnp.concatenate([p["ssd_conv_w"], p["ssd_conv_b"][:, None, :]], axis=1), 8),
        cw_lru=pad_rows(jnp.concatenate([p["lru_conv_w"], p["lru_conv_b"][:, None, :]], axis=1), 8),
        vec=pad_rows(vec, 16), bg=pad_rows(p["gla_b_gate"][:, None, :], 8),
        w2=pad_rows(p["gla_w_gate2"], 128).astype(BF16),
        wa=bd(p["lru_w_a"]), wx=bd(p["lru_w_x"]), w_out=p["w_out"].astype(BF16))


def _rope_tables(pos):
    half = HEAD_DIM // 2
    inv = ROPE_BASE ** (-np.arange(half, dtype=np.float64) / half)
    ang = np.asarray(pos, np.float64)[:, None] * inv[None, :]
    cos, sin = np.cos(ang), np.sin(ang)
    cos = np.tile(np.concatenate([cos, cos], axis=1), (1, N_HEADS)).astype(np.float32)
    sin = np.tile(np.concatenate([-sin, sin], axis=1), (1, N_HEADS)).astype(np.float32)
    return cos, sin


def _prompt_mixer(h, cos, sin, layer, wts, consts, tq):
    b, l, _ = h.shape
    weights = [wts[n] for n in WEIGHT_NAMES]
    cvals = [consts[n] for n in CONST_NAMES]
    row = lambda w: pl.BlockSpec((None, tq, w), lambda bi, ti: (bi, ti, 0))
    per_seq = lambda r, w: pl.BlockSpec((None, r, w), lambda bi, ti: (bi, 0, 0))
    out_shapes = (jax.ShapeDtypeStruct((b, l, D_MODEL), F32),
                  jax.ShapeDtypeStruct((b, 8, SSD_CONV_DIM), F32),
                  jax.ShapeDtypeStruct((b, 8, GROUP_WIDTH), F32),
                  jax.ShapeDtypeStruct((b, 8, GROUP_WIDTH), F32),
                  jax.ShapeDtypeStruct((b, 256, 256), F32),
                  jax.ShapeDtypeStruct((b, 256, 256), F32),
                  jax.ShapeDtypeStruct((b, 128, 256), F32))
    return pl.pallas_call(
        functools.partial(_mixer_kernel, n_chunks=tq // CHUNK, level_sizes=_level_sizes(GLA_CHUNK)),
        out_shape=out_shapes,
        grid=(b, l // tq),
        in_specs=[row(D_MODEL),
                  pl.BlockSpec((tq, GROUP_WIDTH), lambda bi, ti: (ti, 0)),
                  pl.BlockSpec((tq, GROUP_WIDTH), lambda bi, ti: (ti, 0))]
                 + [_layer_spec(a, layer, 2) for a in weights] + [_full_spec(a, 2) for a in cvals],
        out_specs=(row(D_MODEL), per_seq(8, SSD_CONV_DIM), per_seq(8, GROUP_WIDTH),
                   per_seq(8, GROUP_WIDTH), per_seq(256, 256), per_seq(256, 256), per_seq(128, 256)),
        scratch_shapes=[pltpu.VMEM((8 + tq, SSD_CONV_DIM), F32), pltpu.VMEM((8 + tq, GROUP_WIDTH), F32)],
        compiler_params=pltpu.CompilerParams(
            dimension_semantics=("arbitrary", "arbitrary"), vmem_limit_bytes=VMEM_LIMIT_BYTES),
        name="prompt_mixer",
    )(h, cos, sin, *weights, *cvals)


def _sample_pre(hs, cos, sin, cst_ssd, cst_lru, h0, layer, wts):
    t, b = SAMPLE_T, SAMPLE_B
    weights = [wts[n] for n in WEIGHT_NAMES]
    fm = lambda f: jax.ShapeDtypeStruct((t, f, b), F32)
    out_shapes = (fm(256),) * 7 + (fm(128), fm(128), fm(256), fm(128)) + (
        jax.ShapeDtypeStruct((t * b, 1280), F32),
        jax.ShapeDtypeStruct((3, b, SSD_CONV_DIM), F32), jax.ShapeDtypeStruct((3, b, GROUP_WIDTH), F32),
        jax.ShapeDtypeStruct((b, GROUP_WIDTH), F32))
    lay = lambda a: pl.BlockSpec((None,) + a.shape[1:], lambda i: (layer,) + (0,) * (a.ndim - 1))
    return pl.pallas_call(
        _sample_pre_kernel,
        out_shape=out_shapes,
        grid=(1,),
        in_specs=[_full_spec(hs, 1), _full_spec(cos, 1), _full_spec(sin, 1), lay(cst_ssd), lay(cst_lru), lay(h0)]
                 + [_layer_spec(a, layer, 1) for a in weights],
        out_specs=tuple(pl.BlockSpec(o.shape, lambda i, n=len(o.shape): (0,) * n) for o in out_shapes),
        compiler_params=pltpu.CompilerParams(
            dimension_semantics=("arbitrary",), vmem_limit_bytes=VMEM_LIMIT_BYTES),
        name="sample_pre",
    )(hs, cos, sin, cst_ssd, cst_lru, h0, *weights)


def _sample_state(ops, gam, states, new_states, layer):
    t, b = SAMPLE_T, SAMPLE_B
    ssd_q, ssd_k, ssd_v, ssd_a, ret_q, ret_k, ret_v, gla_q, gla_k, gla_v, gla_g = ops
    first = new_states is None
    assert first == (layer == 0)
    aliased = [] if first else list(new_states)
    blk = lambda f: pl.BlockSpec((t, f // N_HEADS, b), lambda h: (0, h, 0))
    st = lambda dk: pl.BlockSpec((None, None, dk, HEAD_DIM, b), lambda h: (layer, h, 0, 0, 0))
    operands = [ssd_q, ssd_k, ssd_v, ssd_a, ret_q, ret_k, ret_v, gam, gla_q, gla_k, gla_v, gla_g, *states]
    in_specs = ([blk(256)] * 7 + [pl.BlockSpec((None, 8, b), lambda h: (h, 0, 0))]
                + [blk(128), blk(128), blk(256), blk(128), st(64), st(64), st(32)]
                + [pl.BlockSpec(memory_space=pl.ANY) for _ in aliased])
    out_shapes = (jax.ShapeDtypeStruct((t, 256, b), F32),) * 3 + tuple(
        jax.ShapeDtypeStruct(s.shape, F32) for s in states)
    depth = states[0].shape[0]
    st_all = lambda dk: pl.BlockSpec((depth, None, dk, HEAD_DIM, b), lambda h: (0, h, 0, 0, 0))
    st_out = st_all if first else st
    return pl.pallas_call(
        functools.partial(_sample_state_kernel, first_layer=first),
        out_shape=out_shapes,
        grid=(N_HEADS,),
        in_specs=in_specs,
        out_specs=(blk(256), blk(256), blk(256), st_out(64), st_out(64), st_out(32)),
        input_output_aliases={len(operands) + k: 3 + k for k in range(len(aliased))},
        compiler_params=pltpu.CompilerParams(
            dimension_semantics=("arbitrary",), vmem_limit_bytes=VMEM_LIMIT_BYTES),
        name="sample_state",
    )(*operands, *aliased)


def _sample_post(hs, side, o_ssd, o_ret, o_gla, layer, wts, mavg, fw, norm_final, final_norm):
    lay = [wts["vec"], wts["w_out"]]
    ffn_w = [fw[n] for n in ("norm_ffn", "wg", "wu", "wd")]
    acts = [hs, side, o_ssd, o_ret, o_gla]
    return pl.pallas_call(
        functools.partial(_sample_post_kernel, final_norm=final_norm),
        out_shape=jax.ShapeDtypeStruct(hs.shape, F32),
        grid=(1,),
        in_specs=[_full_spec(a, 1) for a in acts] + [_layer_spec(a, layer, 1) for a in lay]
                 + [_full_spec(mavg, 1)] + [_layer_spec(a, layer, 1) for a in ffn_w]
                 + [_full_spec(norm_final, 1)],
        out_specs=_full_spec(hs, 1),
        compiler_params=pltpu.CompilerParams(
            dimension_semantics=("arbitrary",), vmem_limit_bytes=VMEM_LIMIT_BYTES),
        name="sample_post",
    )(*acts, *lay, mavg, *ffn_w, norm_final)


def _diag_blocks(s, dk):
    return jnp.stack([s[:, h * dk:(h + 1) * dk, h * HEAD_DIM:(h + 1) * HEAD_DIM]
                      for h in range(N_HEADS)], axis=1)


def kernel(x_prompt, x_sample, state_ssd_conv, state_ssd, state_ret, state_gla, state_lru_conv, state_lru,
           norm_mix, w_in, ssd_conv_w, ssd_conv_b, ssd_dt_bias, ssd_a_log, ssd_d, ssd_norm, ret_norm,
           gla_w_gate2, gla_b_gate, gla_norm, lru_conv_w, lru_conv_b, lru_w_a, lru_b_a, lru_w_x, lru_b_x,
           lru_lambda, w_out, norm_ffn, w_gate, w_up, w_down, norm_final):
    params = dict(norm_mix=norm_mix, w_in=w_in, ssd_conv_w=ssd_conv_w, ssd_conv_b=ssd_conv_b,
                  ssd_dt_bias=ssd_dt_bias, ssd_a_log=ssd_a_log, ssd_d=ssd_d, ssd_norm=ssd_norm,
                  ret_norm=ret_norm, gla_w_gate2=gla_w_gate2, gla_b_gate=gla_b_gate, gla_norm=gla_norm,
                  lru_conv_w=lru_conv_w, lru_conv_b=lru_conv_b, lru_w_a=lru_w_a, lru_b_a=lru_b_a,
                  lru_w_x=lru_w_x, lru_b_x=lru_b_x, lru_lambda=lru_lambda, w_out=w_out)
    depth = w_in.shape[0]
    bp, lp, _ = x_prompt.shape
    bs, ls, _ = x_sample.shape
    assert (bs, ls) == (SAMPLE_B, SAMPLE_T) and lp % CHUNK == 0
    tq = 512 if lp % 512 == 0 else CHUNK
    tm_p = next(t for t in (1024, 512, CHUNK) if (bp * lp) % t == 0)

    wts = _pack_weights(params)
    fw = dict(norm_ffn=norm_ffn[:, None, :], wg=w_gate.astype(BF16), wu=w_up.astype(BF16),
              wd=w_down.astype(BF16))
    consts = _const_arrays()
    cos_p, sin_p = (jnp.asarray(a) for a in _rope_tables(np.arange(lp)))
    cos_s, sin_s = (jnp.asarray(np.repeat(a, bs, axis=0))
                    for a in _rope_tables(PAST_LEN + np.arange(ls)))
    nfin = norm_final[None]
    gam = jnp.asarray(np.broadcast_to(np.exp(_ret_log_gamma())[:, None, None], (N_HEADS, 8, bs)), F32)
    cst_ssd = jnp.transpose(state_ssd_conv, (0, 2, 1, 3))
    cst_lru = jnp.transpose(state_lru_conv, (0, 2, 1, 3))
    states_s = tuple(jnp.transpose(a, (0, 2, 3, 4, 1)) for a in (state_ssd, state_ret, state_gla))

    hp = x_prompt
    hs = jnp.transpose(x_sample, (1, 0, 2)).reshape(ls * bs, D_MODEL)
    new_p = [[] for _ in range(6)]
    new_s = [[] for _ in range(3)]
    new_states_s = None
    for i in range(depth):
        last = i == depth - 1

        hmid, tail_ssd, tail_lru, hl, s_ssd, s_ret, s_gla = _prompt_mixer(
            hp, cos_p, sin_p, i, wts, consts, tq)
        hp = _ffn(hmid.reshape(bp * lp, D_MODEL), i, fw, nfin, last, tm_p).reshape(bp, lp, D_MODEL)
        for j, a in enumerate((tail_ssd[:, 5:], _diag_blocks(s_ssd, 64), _diag_blocks(s_ret, 64),
                               _diag_blocks(s_gla, 32), tail_lru[:, 5:], hl[:, 7])):
            new_p[j].append(a)

        *ops, side, nconv_ssd, nconv_lru, nh = _sample_pre(
            hs, cos_s, sin_s, cst_ssd, cst_lru, state_lru, i, wts)
        o_ssd, o_ret, o_gla, *new_states_s = _sample_state(ops, gam, states_s, new_states_s, i)
        hs = _sample_post(hs, side, o_ssd, o_ret, o_gla, i, wts, consts["mavg"], fw, nfin, last)
        for j, a in enumerate((nconv_ssd, nconv_lru, nh)):
            new_s[j].append(a)

    outs_p = [jnp.stack(a, axis=0) for a in new_p]
    s_conv, s_lru_conv, s_lru = [jnp.stack(a, axis=0) for a in new_s]
    n_ssd, n_ret, n_gla = (jnp.transpose(a, (0, 4, 1, 2, 3)) for a in new_states_s)
    y_sample = jnp.transpose(hs.reshape(ls, bs, D_MODEL), (1, 0, 2))
    return (hp, y_sample, *outs_p,
            jnp.transpose(s_conv, (0, 2, 1, 3)), n_ssd, n_ret, n_gla,
            jnp.transpose(s_lru_conv, (0, 2, 1, 3)), s_lru)
```
